```python
import jax, jax.numpy as jnp
from jax import lax
import numpy as np

D_MODEL = 4096
BATCH = 4
SEQ = 4096
DEPTH = 1

D_RWKV = D_MODEL // 2
RWKV_HEAD = 64
RWKV_HEADS = D_RWKV // RWKV_HEAD
DECAY_RANK = 96
ICLR_RANK = 96
GATE_RANK = 256
D_CONV = D_MODEL // 2
CONV_WIDTH = 31
MEM_LEN = 256
XATTN_HEADS = 4
XATTN_HEAD_DIM = D_MODEL // XATTN_HEADS
D_FF = 4 * D_MODEL
DEEPNORM_ALPHA = float((2 * DEPTH) ** 0.25)
DEEPNORM_BETA = float((8 * DEPTH) ** -0.25)
LN_EPS = 1e-5
GN_EPS = 64e-5

N_RWKV_COLS = 3 * D_RWKV + DECAY_RANK + ICLR_RANK + GATE_RANK
N_CONV_COLS = 2 * D_CONV
N_GATE_COLS = 2 * D_MODEL
D_IN = N_RWKV_COLS + N_CONV_COLS + N_GATE_COLS
RWKV_SPLITS = [D_RWKV, 2 * D_RWKV, 3 * D_RWKV, 3 * D_RWKV + DECAY_RANK, 3 * D_RWKV + DECAY_RANK + ICLR_RANK]

kernel_name = "rwkv7_conformer_gated_hybrid_deepnorm"


def layer_norm(x, g, b, eps=LN_EPS):
    xf = x.astype(jnp.float32)
    mu = jnp.mean(xf, -1, keepdims=True)
    var = jnp.mean(jnp.square(xf - mu), -1, keepdims=True)
    return ((xf - mu) * lax.rsqrt(var + eps) * g.astype(jnp.float32) + b.astype(jnp.float32)).astype(x.dtype)


def token_shift(z, mu):
    prev = jnp.pad(z, ((0, 0), (1, 0), (0, 0)))[:, :-1]
    return z + (prev - z) * mu


def rwkv7_scan(r, decay, k, v, a, b):
    B, S, H, N = r.shape

    def step(state, inp):
        r_t, w_t, k_t, v_t, a_t, b_t = inp
        sa = jnp.einsum('bhvk,bhk->bhv', state, a_t)
        state = (state * w_t[:, :, None, :] + sa[..., None] * b_t[:, :, None, :]
                 + v_t[..., None] * k_t[:, :, None, :])
        return state, jnp.einsum('bhvk,bhk->bhv', state, r_t)

    s0 = jnp.zeros((B, H, N, N), jnp.float32)
    xs = tuple(jnp.swapaxes(t, 0, 1) for t in (r, decay, k, v, a, b))
    _, out = lax.scan(step, s0, xs)
    return jnp.swapaxes(out, 0, 1)


def rwkv7_time_mix(z, shift_mix, w0, w_up, a0, a_up, g_up, k_k, k_a, r_k, gn_g, gn_b):
    B, S, _ = z.shape
    f32 = jnp.float32
    z = token_shift(z, shift_mix)
    r, k, v, dw, da, dg = jnp.split(z, RWKV_SPLITS, axis=-1)
    w = -jax.nn.softplus(-(w0 + jnp.tanh(dw) @ w_up)) - 0.5
    decay = jnp.exp(-jnp.exp(w.astype(f32)))
    a = jax.nn.sigmoid(a0 + da @ a_up)
    g = jax.nn.sigmoid(dg) @ g_up
    heads = lambda t: t.reshape(B, S, RWKV_HEADS, RWKV_HEAD).astype(f32)
    kk = heads(k * k_k)
    kk = kk / jnp.maximum(jnp.sqrt(jnp.sum(kk * kk, -1, keepdims=True)), 1e-12)
    k = k * (1.0 + (a - 1.0) * k_a)
    rh, kh, vh, ah = heads(r), heads(k), heads(v), heads(a)
    o = rwkv7_scan(rh, heads(decay), kh, vh, -kk, kk * ah)
    mu = jnp.mean(o, -1, keepdims=True)
    var = jnp.mean(jnp.square(o - mu), -1, keepdims=True)
    o = ((o - mu) * lax.rsqrt(var + GN_EPS)).reshape(B, S, D_RWKV) * gn_g.astype(f32) + gn_b.astype(f32)
    bonus = jnp.sum(rh * kh * r_k.astype(f32), -1, keepdims=True) * vh
    o = o + bonus.reshape(B, S, D_RWKV)
    return (o * g.astype(f32)).astype(z.dtype)


def conformer_conv(z, conv_w, conv_b, ln_g, ln_b):
    u = z[..., :D_CONV] * jax.nn.sigmoid(z[..., D_CONV:])
    u = lax.conv_general_dilated(u, conv_w, window_strides=(1,), padding=[(CONV_WIDTH - 1, 0)],
                                 dimension_numbers=('NWC', 'WIO', 'NWC'),
                                 feature_group_count=D_CONV) + conv_b
    return jax.nn.silu(layer_norm(u, ln_g, ln_b))


def memory_cross_attention(h, mem_n, wq, wk, wv, wo):
    B, S, _ = h.shape
    M = mem_n.shape[1]
    q = (h @ wq).reshape(B, S, XATTN_HEADS, XATTN_HEAD_DIM)
    k = (mem_n @ wk).reshape(B, M, XATTN_HEADS, XATTN_HEAD_DIM)
    v = (mem_n @ wv).reshape(B, M, XATTN_HEADS, XATTN_HEAD_DIM)
    s = jnp.einsum('bshd,bmhd->bhsm', q, k).astype(jnp.float32) * (XATTN_HEAD_DIM ** -0.5)
    p = jax.nn.softmax(s, axis=-1).astype(v.dtype)
    o = jnp.einsum('bhsm,bmhd->bshd', p, v).reshape(B, S, D_MODEL)
    return o @ wo


def setup_inputs(seed: int = 0) -> dict:
    key = jax.random.key(seed)
    ks = iter(jax.random.split(key, 48))
    L = DEPTH
    nrm = lambda shape, scale: jax.random.normal(next(ks), shape, jnp.float32) * scale
    gain = lambda shape: 1.0 + nrm(shape, 0.02)
    beta = DEEPNORM_BETA
    lin = jnp.linspace(0.0, 1.0, D_RWKV, dtype=jnp.float32)
    w0 = -7.0 + 5.0 * lin ** 0.85 + 0.5
    return {
        "x": nrm((BATCH, SEQ, D_MODEL), 1.0),
        "mem": nrm((BATCH, MEM_LEN, D_MODEL), 1.0),
        "w_in": nrm((L, D_MODEL, D_IN), D_MODEL ** -0.5),
        "rwkv_shift_mix": jax.random.uniform(next(ks), (L, N_RWKV_COLS), jnp.float32),
        "rwkv_w0": w0[None, :] + nrm((L, D_RWKV), 0.05),
        "rwkv_w_up": nrm((L, DECAY_RANK, D_RWKV), 0.1 * DECAY_RANK ** -0.5),
        "rwkv_a0": nrm((L, D_RWKV), 0.1),
        "rwkv_a_up": nrm((L, ICLR_RANK, D_RWKV), 0.1 * ICLR_RANK ** -0.5),
        "rwkv_g_up": nrm((L, GATE_RANK, D_RWKV), GATE_RANK ** -0.5),
        "rwkv_k_k": 0.85 + nrm((L, D_RWKV), 0.05),
        "rwkv_k_a": 1.0 + nrm((L, D_RWKV), 0.05),
        "rwkv_r_k": nrm((L, RWKV_HEADS, RWKV_HEAD), 0.1),
        "rwkv_gn_g": gain((L, D_RWKV)),
        "rwkv_gn_b": nrm((L, D_RWKV), 0.02),
        "conv_w": nrm((L, CONV_WIDTH, 1, D_CONV), CONV_WIDTH ** -0.5),
        "conv_b": nrm((L, D_CONV), 0.02),
        "conv_ln_g": gain((L, D_CONV)),
        "conv_ln_b": nrm((L, D_CONV), 0.02),
        "proj_rwkv": nrm((L, D_RWKV, D_MODEL), beta * D_RWKV ** -0.5),
        "proj_conv": nrm((L, D_CONV, D_MODEL), beta * D_CONV ** -0.5),
        "w_out": nrm((L, D_MODEL, D_MODEL), beta * D_MODEL ** -0.5),
        "ln1_g": gain((L, D_MODEL)),
        "ln1_b": nrm((L, D_MODEL), 0.02),
        "ln_mem_g": gain((D_MODEL,)),
        "ln_mem_b": nrm((D_MODEL,), 0.02),
        "xattn_wq": nrm((L, D_MODEL, D_MODEL), D_MODEL ** -0.5),
        "xattn_wk": nrm((L, D_MODEL, D_MODEL), D_MODEL ** -0.5),
        "xattn_wv": nrm((L, D_MODEL, D_MODEL), beta * D_MODEL ** -0.5),
        "xattn_wo": nrm((L, D_MODEL, D_MODEL), beta * D_MODEL ** -0.5),
        "ln2_g": gain((L, D_MODEL)),
        "ln2_b": nrm((L, D_MODEL), 0.02),
        "mlp_w1": nrm((L, D_MODEL, D_FF), beta * D_MODEL ** -0.5),
        "mlp_w2": nrm((L, D_FF, D_MODEL), beta * D_FF ** -0.5),
        "ln3_g": gain((L, D_MODEL)),
        "ln3_b": nrm((L, D_MODEL), 0.02),
    }


def reference(x, mem, w_in, rwkv_shift_mix, rwkv_w0, rwkv_w_up, rwkv_a0, rwkv_a_up, rwkv_g_up,
              rwkv_k_k, rwkv_k_a, rwkv_r_k, rwkv_gn_g, rwkv_gn_b, conv_w, conv_b, conv_ln_g,
              conv_ln_b, proj_rwkv, proj_conv, w_out, ln1_g, ln1_b, ln_mem_g, ln_mem_b,
              xattn_wq, xattn_wk, xattn_wv, xattn_wo, ln2_g, ln2_b, mlp_w1, mlp_w2, ln3_g, ln3_b):
    alpha = DEEPNORM_ALPHA
    mem_n = layer_norm(mem, ln_mem_g, ln_mem_b)
    h = x
    for l in range(DEPTH):
        z = h @ w_in[l]
        z_rwkv = z[..., :N_RWKV_COLS]
        z_conv = z[..., N_RWKV_COLS:N_RWKV_COLS + N_CONV_COLS]
        z_gate = z[..., N_RWKV_COLS + N_CONV_COLS:]
        o_r = rwkv7_time_mix(z_rwkv, rwkv_shift_mix[l], rwkv_w0[l], rwkv_w_up[l], rwkv_a0[l],
                             rwkv_a_up[l], rwkv_g_up[l], rwkv_k_k[l], rwkv_k_a[l], rwkv_r_k[l],
                             rwkv_gn_g[l], rwkv_gn_b[l])
        o_c = conformer_conv(z_conv, conv_w[l], conv_b[l], conv_ln_g[l], conv_ln_b[l])
        gate_r = jax.nn.sigmoid(z_gate[..., :D_MODEL])
        gate_c = jax.nn.sigmoid(z_gate[..., D_MODEL:])
        merged = gate_r * (o_r @ proj_rwkv[l]) + gate_c * (o_c @ proj_conv[l])
        h = layer_norm(alpha * h + merged @ w_out[l], ln1_g[l], ln1_b[l])
        ca = memory_cross_attention(h, mem_n, xattn_wq[l], xattn_wk[l], xattn_wv[l], xattn_wo[l])
        h = layer_norm(alpha * h + ca, ln2_g[l], ln2_b[l])
        ff = jnp.square(jax.nn.relu(h @ mlp_w1[l])) @ mlp_w2[l]
        h = layer_norm(alpha * h + ff, ln3_g[l], ln3_b[l])
    return h
```

```python
import functools

import jax
import jax.numpy as jnp
from jax import lax
from jax.experimental import pallas as pl
from jax.experimental.pallas import tpu as pltpu

F32 = jnp.float32
BF16 = jnp.bfloat16

LANES = 128
RWKV_HEAD = 64
PAIR = 2 * RWKV_HEAD
CHUNK = 64
CONV_HALO = 32
XATTN_HEADS = 4
LN_EPS = 1e-5
GN_EPS = 64e-5
VMEM_LIMIT = 56 * 1024 * 1024

_NN = (((1,), (0,)), ((), ()))
_NT = (((1,), (1,)), ((), ()))
_TN = (((0,), (0,)), ((), ()))


def _params(*sem):
    return pltpu.CompilerParams(dimension_semantics=sem, vmem_limit_bytes=VMEM_LIMIT)


def _dot(a, b, dims=_NN):
    return lax.dot_general(a, b, dims, preferred_element_type=F32)


def _split2(x):
    hi = x.astype(BF16)
    lo = (x - hi.astype(F32)).astype(BF16)
    return hi, lo


def _dot3(a, b, dims=_NN):
    ah, al = _split2(a)
    bh, bl = _split2(b)
    return _dot(ah, bh, dims) + (_dot(ah, bl, dims) + _dot(al, bh, dims))


def _dot_exact_rhs(a, b_bf16):
    hi = a.astype(BF16)
    r1 = a - hi.astype(F32)
    mid = r1.astype(BF16)
    lo = (r1 - mid.astype(F32)).astype(BF16)
    return _dot(hi, b_bf16) + (_dot(mid, b_bf16) + _dot(lo, b_bf16))


def _dot_exact_lhs(a_bf16, b):
    hi = b.astype(BF16)
    r1 = b - hi.astype(F32)
    mid = r1.astype(BF16)
    lo = (r1 - mid.astype(F32)).astype(BF16)
    return _dot(a_bf16, hi) + (_dot(a_bf16, mid) + _dot(a_bf16, lo))


def _sigmoid(x):
    return 1.0 / (1.0 + jnp.exp(-x))


def _layer_norm_rows(v, g, b):
    mu = jnp.mean(v, axis=-1, keepdims=True)
    d = v - mu
    var = jnp.mean(d * d, axis=-1, keepdims=True)
    return d * lax.rsqrt(var + LN_EPS) * g + b


def _tile(n, pref):
    t = min(n, pref)
    assert n % t == 0, (n, pref)
    return t


def _mm_shift_kernel(x_ref, w_ref, mu_ref, o_ref, carry_ref, *, tiles_per_seq):
    m = pl.program_id(1)
    z = _dot(x_ref[...], w_ref[...])
    tm = z.shape[0]

    @pl.when(m % tiles_per_seq == 0)
    def _():
        carry_ref[...] = jnp.zeros_like(carry_ref)

    prev_row = carry_ref[0:1, :]
    row = lax.broadcasted_iota(jnp.int32, z.shape, 0)
    prev = jnp.where(row == 0, prev_row, pltpu.roll(z, 1, axis=0))
    o_ref[...] = z + (prev - z) * mu_ref[...]
    carry_ref[0:1, :] = z[tm - 1:tm, :]


def _mm_shift(x, w, mu, seq, tm=1024, tn=512):
    M, K = x.shape
    N = w.shape[1]
    tm = _tile(seq, tm)
    tn = _tile(N, tn)
    return pl.pallas_call(
        functools.partial(_mm_shift_kernel, tiles_per_seq=seq // tm),
        out_shape=jax.ShapeDtypeStruct((M, N), F32),
        grid=(N // tn, M // tm),
        in_specs=[
            pl.BlockSpec((tm, K), lambda n, m: (m, 0)),
            pl.BlockSpec((K, tn), lambda n, m: (0, n)),
            pl.BlockSpec((1, tn), lambda n, m: (0, n)),
        ],
        out_specs=pl.BlockSpec((tm, tn), lambda n, m: (m, n)),
        scratch_shapes=[pltpu.VMEM((8, tn), F32)],
        compiler_params=_params("parallel", "arbitrary"),
    )(x, w, mu)


def _mm_glu_kernel(x_ref, wa_ref, wb_ref, o_ref):
    x = x_ref[...]
    o_ref[...] = _dot(x, wa_ref[...]) * _sigmoid(_dot(x, wb_ref[...]))


def _mm_glu(x, wa, wb, tm=1024, tn=256):
    M, K = x.shape
    N = wa.shape[1]
    tm = _tile(M, tm)
    tn = _tile(N, tn)
    return pl.pallas_call(
        _mm_glu_kernel,
        out_shape=jax.ShapeDtypeStruct((M, N), F32),
        grid=(N // tn, M // tm),
        in_specs=[
            pl.BlockSpec((tm, K), lambda n, m: (m, 0)),
            pl.BlockSpec((K, tn), lambda n, m: (0, n)),
            pl.BlockSpec((K, tn), lambda n, m: (0, n)),
        ],
        out_specs=pl.BlockSpec((tm, tn), lambda n, m: (m, n)),
        compiler_params=_params("parallel", "arbitrary"),
    )(x, wa, wb)


def _mm_plain_kernel(x_ref, w_ref, o_ref):
    o_ref[...] = _dot(x_ref[...], w_ref[...]).astype(o_ref.dtype)


def _mm_plain(x, w, out_dtype, tm=1024, tn=512):
    M, K = x.shape
    N = w.shape[1]
    tm = _tile(M, tm)
    tn = _tile(N, tn)
    return pl.pallas_call(
        _mm_plain_kernel,
        out_shape=jax.ShapeDtypeStruct((M, N), out_dtype),
        grid=(N // tn, M // tm),
        in_specs=[
            pl.BlockSpec((tm, K), lambda n, m: (m, 0)),
            pl.BlockSpec((K, tn), lambda n, m: (0, n)),
        ],
        out_specs=pl.BlockSpec((tm, tn), lambda n, m: (m, n)),
        compiler_params=_params("parallel", "arbitrary"),
    )(x, w)


def _mm_merge_kernel(x_ref, orw_ref, ocv_ref, wgr_ref, wgc_ref, pr_ref, pc_ref, o_ref):
    x = x_ref[...]
    gate_r = _sigmoid(_dot(x, wgr_ref[...]))
    gate_c = _sigmoid(_dot(x, wgc_ref[...]))
    merged = gate_r * _dot(orw_ref[...], pr_ref[...]) + gate_c * _dot(ocv_ref[...], pc_ref[...])
    o_ref[...] = merged.astype(o_ref.dtype)


def _mm_merge(x, o_r, o_c, wgr, wgc, p_r, p_c, tm=512, tn=512):
    M, K = x.shape
    Kb = o_r.shape[1]
    N = wgr.shape[1]
    tm = _tile(M, tm)
    tn = _tile(N, tn)
    row = lambda k: pl.BlockSpec((tm, k), lambda n, m: (m, 0))
    col = lambda k: pl.BlockSpec((k, tn), lambda n, m: (0, n))
    return pl.pallas_call(
        _mm_merge_kernel,
        out_shape=jax.ShapeDtypeStruct((M, N), BF16),
        grid=(N // tn, M // tm),
        in_specs=[row(K), row(Kb), row(Kb), col(K), col(K), col(Kb), col(Kb)],
        out_specs=pl.BlockSpec((tm, tn), lambda n, m: (m, n)),
        compiler_params=_params("parallel", "arbitrary"),
    )(x, o_r, o_c, wgr, wgc, p_r, p_c)


def _ln_inplace(o_ref, ob_ref, g_ref, b_ref, rows=32):
    g = g_ref[...]
    b = b_ref[...]

    def body(i, _):
        sl = pl.ds(pl.multiple_of(i * rows, rows), rows)
        y = _layer_norm_rows(o_ref[sl, :], g, b)
        o_ref[sl, :] = y
        if ob_ref is not None:
            ob_ref[sl, :] = y.astype(ob_ref.dtype)
        return 0

    lax.fori_loop(0, o_ref.shape[0] // rows, body, 0)


def _mm_res_ln_kernel(y_ref, w_ref, res_ref, g_ref, b_ref, o_ref, ob_ref, *, alpha, tn):
    n = pl.program_id(1)
    col = pl.ds(pl.multiple_of(n * tn, tn), tn)
    o_ref[:, col] = alpha * res_ref[...] + _dot(y_ref[...], w_ref[...])

    @pl.when(n == pl.num_programs(1) - 1)
    def _():
        _ln_inplace(o_ref, ob_ref, g_ref, b_ref)


def _mm_res_ln(y, w, res, g, b, alpha, tm=512, tn=512):
    M, K = y.shape
    N = w.shape[1]
    tm = _tile(M, tm)
    tn = _tile(N, tn)
    return pl.pallas_call(
        functools.partial(_mm_res_ln_kernel, alpha=alpha, tn=tn),
        out_shape=(jax.ShapeDtypeStruct((M, N), F32), jax.ShapeDtypeStruct((M, N), BF16)),
        grid=(M // tm, N // tn),
        in_specs=[
            pl.BlockSpec((tm, K), lambda m, n: (m, 0)),
            pl.BlockSpec((K, tn), lambda m, n: (0, n)),
            pl.BlockSpec((tm, tn), lambda m, n: (m, n)),
            pl.BlockSpec((1, N), lambda m, n: (0, 0)),
            pl.BlockSpec((1, N), lambda m, n: (0, 0)),
        ],
        out_specs=(pl.BlockSpec((tm, N), lambda m, n: (m, 0)),
                   pl.BlockSpec((tm, N), lambda m, n: (m, 0))),
        compiler_params=_params("parallel", "arbitrary"),
    )(y, w, res, g, b)


def _mlp_ln_kernel(x_ref, w1_ref, w2_ref, res_ref, g_ref, b_ref, o_ref, *, alpha, tn):
    k = pl.program_id(1)
    h = jnp.maximum(_dot(x_ref[...], w1_ref[...]), 0.0)
    h = (h * h).astype(BF16)
    n_slabs = o_ref.shape[1] // tn

    @pl.when(k == 0)
    def _():
        for j in range(n_slabs):
            sl = slice(j * tn, (j + 1) * tn)
            o_ref[:, sl] = alpha * res_ref[:, sl] + _dot(h, w2_ref[:, sl])

    @pl.when(k > 0)
    def _():
        for j in range(n_slabs):
            sl = slice(j * tn, (j + 1) * tn)
            o_ref[:, sl] += _dot(h, w2_ref[:, sl])

    @pl.when(k == pl.num_programs(1) - 1)
    def _():
        _ln_inplace(o_ref, None, g_ref, b_ref)


def _mlp_ln(x, w1, w2, res, g, b, alpha, tm=512, tk=512, tn=512):
    M, D = x.shape
    FF = w1.shape[1]
    tm = _tile(M, tm)
    tk = _tile(FF, tk)
    tn = _tile(D, tn)
    single = pl.Buffered(1)
    return pl.pallas_call(
        functools.partial(_mlp_ln_kernel, alpha=alpha, tn=tn),
        out_shape=jax.ShapeDtypeStruct((M, D), F32),
        grid=(M // tm, FF // tk),
        in_specs=[
            pl.BlockSpec((tm, D), lambda m, k: (m, 0), pipeline_mode=single),
            pl.BlockSpec((D, tk), lambda m, k: (0, k)),
            pl.BlockSpec((tk, D), lambda m, k: (k, 0)),
            pl.BlockSpec((tm, D), lambda m, k: (m, 0), pipeline_mode=single),
            pl.BlockSpec((1, D), lambda m, k: (0, 0)),
            pl.BlockSpec((1, D), lambda m, k: (0, 0)),
        ],
        out_specs=pl.BlockSpec((tm, D), lambda m, k: (m, 0)),
        compiler_params=_params("parallel", "arbitrary"),
    )(x, w1, w2, res, g, b)


def _ln_rows_kernel(x_ref, g_ref, b_ref, o_ref):
    o_ref[...] = _layer_norm_rows(x_ref[...], g_ref[...], b_ref[...]).astype(o_ref.dtype)


def _ln_rows(x, g, b, out_dtype, tm=128):
    M, D = x.shape
    tm = _tile(M, tm)
    return pl.pallas_call(
        _ln_rows_kernel,
        out_shape=jax.ShapeDtypeStruct((M, D), out_dtype),
        grid=(M // tm,),
        in_specs=[pl.BlockSpec((tm, D), lambda m: (m, 0)),
                  pl.BlockSpec((1, D), lambda m: (0, 0)),
                  pl.BlockSpec((1, D), lambda m: (0, 0))],
        out_specs=pl.BlockSpec((tm, D), lambda m: (m, 0)),
        compiler_params=_params("parallel"),
    )(x, g, b)


def _lora_act_kernel(z_ref, o_ref):
    o_ref[:, 0:LANES] = jnp.tanh(z_ref[:, 0:LANES])
    o_ref[:, LANES:2 * LANES] = z_ref[:, LANES:2 * LANES]
    o_ref[:, 2 * LANES:] = _sigmoid(z_ref[:, 2 * LANES:])


def _lora_act(zs, col_block, tm=1024):
    M = zs.shape[0]
    W = 4 * LANES
    tm = _tile(M, tm)
    return pl.pallas_call(
        _lora_act_kernel,
        out_shape=jax.ShapeDtypeStruct((M, W), F32),
        grid=(M // tm,),
        in_specs=[pl.BlockSpec((tm, W), lambda m: (m, col_block))],
        out_specs=pl.BlockSpec((tm, W), lambda m: (m, 0)),
        compiler_params=_params("parallel"),
    )(zs)


def _rwkv_kernel(zr_ref, zk_ref, zv_ref, l_ref, wup_ref, aup_ref, gup_ref, prm_ref, o_ref, h_ref,
                 *, n_chunks):
    @pl.when(pl.program_id(2) == 0)
    def _():
        h_ref[...] = jnp.zeros_like(h_ref)

    C = CHUNK
    lane = lax.broadcasted_iota(jnp.int32, (1, PAIR), 1)
    m0 = (lane < RWKV_HEAD).astype(F32)
    m1 = 1.0 - m0
    ri = lax.broadcasted_iota(jnp.int32, (PAIR, PAIR), 0)
    ci = lax.broadcasted_iota(jnp.int32, (PAIR, PAIR), 1)
    strict = ci < ri
    lower = ci <= ri
    head_bits = RWKV_HEAD.bit_length() - 1
    same_head = jnp.right_shift(ri, head_bits) == jnp.right_shift(ci, head_bits)
    eye = (ri == ci).astype(F32)
    head_ones = same_head.astype(BF16)
    rc = lax.broadcasted_iota(jnp.int32, (C, C), 0)
    cc = lax.broadcasted_iota(jnp.int32, (C, C), 1)
    tril_c = (cc <= rc).astype(BF16)

    prm = prm_ref[...]
    w0, a0, k_k, k_a, r_k, gn_g, gn_b = (prm[i:i + 1, :] for i in range(7))
    w_up = wup_ref[...]
    a_up = aup_ref[...]
    g_up = gup_ref[...]

    def stack(x):
        return jnp.concatenate([x * m0, x * m1], axis=0)

    def head_sum(x):
        return _dot_exact_rhs(x, head_ones)

    def body(c, _):
        rows = pl.ds(pl.multiple_of(c * C, C), C)
        r = zr_ref[rows, :]
        k = zk_ref[rows, :]
        v = zv_ref[rows, :]
        lora = l_ref[rows, :]

        w_raw = w0 + _dot3(lora[:, 0:LANES], w_up)
        a_raw = a0 + _dot3(lora[:, LANES:2 * LANES], a_up)
        g = _dot3(lora[:, 2 * LANES:], g_up)
        w = -(jnp.maximum(-w_raw, 0.0) + jnp.log(1.0 + jnp.exp(-jnp.abs(w_raw)))) - 0.5
        wl = -jnp.exp(w)
        a_sig = _sigmoid(a_raw)
        kk = k * k_k
        kk = kk / jnp.maximum(jnp.sqrt(head_sum(kk * kk)), 1e-12)
        k2 = k * (1.0 + (a_sig - 1.0) * k_a)
        a_s = -kk
        b_s = kk * a_sig

        cum = _dot_exact_lhs(tril_c, wl)
        cum_end = cum[C - 1:C, :]
        e_neg = jnp.exp(-cum)
        e_tail = jnp.exp(cum_end - cum)
        r_t = r * jnp.exp(cum)
        a_t = a_s * jnp.exp(cum - wl)
        b_t = b_s * e_neg
        k_t = k2 * e_neg
        w_end = jnp.exp(cum_end)

        a_st = stack(a_t)
        r_st = stack(r_t)
        v_st = stack(v)
        gram = _dot3(jnp.concatenate([a_st, r_st], axis=0),
                     jnp.concatenate([stack(b_t), stack(k_t)], axis=0), _NT)
        a_ab = jnp.where(strict, gram[:PAIR, :PAIR], 0.0)
        a_ak = jnp.where(strict, gram[:PAIR, PAIR:], 0.0)
        g_rb = jnp.where(lower, gram[PAIR:, :PAIR], 0.0)
        g_rk = jnp.where(lower, gram[PAIR:, PAIR:], 0.0)

        p = a_ab
        t_inv = eye + a_ab
        n_sq = max(C.bit_length() - 2, 0)
        for _ in range(n_sq):
            p = _dot3(p, p)
            t_inv = t_inv + _dot3(t_inv, p)

        h = h_ref[...]
        rhs = _dot3(jnp.concatenate([a_st, a_ak], axis=1), jnp.concatenate([h, v_st], axis=0))
        u = _dot3(t_inv, rhs)
        o_st = _dot3(jnp.concatenate([r_st, g_rb, g_rk], axis=1), jnp.concatenate([h, u, v_st], axis=0))
        o = o_st[:C, :] + o_st[C:, :]

        w_col = jnp.broadcast_to(w_end, (PAIR, PAIR)).T
        h_new = w_col * h + _dot3(jnp.concatenate([stack(b_s * e_tail), stack(k2 * e_tail)], axis=0),
                                  jnp.concatenate([u, v_st], axis=0), _TN)
        h_ref[...] = jnp.where(same_head, h_new, 0.0)

        inv_n = 1.0 / RWKV_HEAD
        mu = head_sum(o) * inv_n
        d = o - mu
        var = head_sum(d * d) * inv_n
        y = d * lax.rsqrt(var + GN_EPS) * gn_g + gn_b
        y = y + head_sum(r * k2 * r_k) * v
        o_ref[rows, :] = (y * g).astype(o_ref.dtype)
        return 0

    lax.fori_loop(0, n_chunks, body, 0)


def _rwkv_mix(zs, lora, w_up, a_up, g_up, prm, batch, seq, d_rwkv, tt=256):
    M = zs.shape[0]
    n_pairs = d_rwkv // PAIR
    tt = _tile(seq, tt)
    nt = seq // tt
    zspec = lambda off: pl.BlockSpec((tt, PAIR), lambda b, p, t: (b * nt + t, off + p))
    wspec = lambda k: pl.BlockSpec((k, PAIR), lambda b, p, t: (0, p))
    return pl.pallas_call(
        functools.partial(_rwkv_kernel, n_chunks=tt // CHUNK),
        out_shape=jax.ShapeDtypeStruct((M, d_rwkv), BF16),
        grid=(batch, n_pairs, nt),
        in_specs=[
            zspec(0), zspec(n_pairs), zspec(2 * n_pairs),
            pl.BlockSpec((tt, 4 * LANES), lambda b, p, t: (b * nt + t, 0)),
            wspec(LANES), wspec(LANES), wspec(2 * LANES), wspec(8),
        ],
        out_specs=pl.BlockSpec((tt, PAIR), lambda b, p, t: (b * nt + t, p)),
        scratch_shapes=[pltpu.VMEM((PAIR, PAIR), F32)],
        compiler_params=_params("parallel", "parallel", "arbitrary"),
    )(zs, zs, zs, lora, w_up, a_up, g_up, prm)


def _conv_kernel(u_ref, w_ref, cb_ref, g_ref, b_ref, o_ref, buf_ref, acc_ref, *, width):
    tt, ch = u_ref.shape
    t = pl.program_id(1)

    @pl.when(t == 0)
    def _():
        buf_ref[0:CONV_HALO, :] = jnp.zeros((CONV_HALO, ch), F32)

    buf_ref[CONV_HALO:, :] = u_ref[...]
    base = CONV_HALO - (width - 1)

    def slab(s, _):
        cols = pl.ds(pl.multiple_of(s * LANES, LANES), LANES)
        acc = jnp.broadcast_to(cb_ref[:, cols], (tt, LANES))
        for j in range(width):
            acc = acc + w_ref[pl.ds(j, 1), cols] * buf_ref[pl.ds(base + j, tt), cols]
        acc_ref[:, cols] = acc
        return 0

    lax.fori_loop(0, ch // LANES, slab, 0)
    buf_ref[0:CONV_HALO, :] = buf_ref[tt:tt + CONV_HALO, :]

    g = g_ref[...]
    b = b_ref[...]
    rows = 32

    def norm(i, _):
        sl = pl.ds(pl.multiple_of(i * rows, rows), rows)
        y = _layer_norm_rows(acc_ref[sl, :], g, b)
        o_ref[sl, :] = (y * _sigmoid(y)).astype(o_ref.dtype)
        return 0

    lax.fori_loop(0, tt // rows, norm, 0)


def _conv_module(u, conv_w, conv_b, ln_g, ln_b, batch, seq, width, tt=256):
    M, ch = u.shape
    tt = _tile(seq, tt)
    nt = seq // tt
    vec = lambda r: pl.BlockSpec((r, ch), lambda b, t: (0, 0))
    return pl.pallas_call(
        functools.partial(_conv_kernel, width=width),
        out_shape=jax.ShapeDtypeStruct((M, ch), BF16),
        grid=(batch, nt),
        in_specs=[pl.BlockSpec((tt, ch), lambda b, t: (b * nt + t, 0)),
                  vec(conv_w.shape[0]), vec(1), vec(1), vec(1)],
        out_specs=pl.BlockSpec((tt, ch), lambda b, t: (b * nt + t, 0)),
        scratch_shapes=[pltpu.VMEM((tt + CONV_HALO, ch), F32), pltpu.VMEM((tt, ch), F32)],
        compiler_params=_params("parallel", "arbitrary"),
    )(u, conv_w, conv_b, ln_g, ln_b)


def _attn_kernel(q_ref, k_ref, v_ref, o_ref, *, heads, scale):
    hd = q_ref.shape[1] // heads
    for h in range(heads):
        sl = slice(h * hd, (h + 1) * hd)
        s = _dot(q_ref[:, sl], k_ref[:, sl], _NT) * scale
        e = jnp.exp(s - jnp.max(s, axis=-1, keepdims=True))
        p = e / jnp.sum(e, axis=-1, keepdims=True)
        o_ref[:, sl] = _dot(p.astype(BF16), v_ref[:, sl]).astype(o_ref.dtype)


def _attention(q, k, v, batch, seq, mem_len, heads, tq=512):
    M, D = q.shape
    tq = _tile(seq, tq)
    nt = seq // tq
    scale = float((D // heads) ** -0.5)
    return pl.pallas_call(
        functools.partial(_attn_kernel, heads=heads, scale=scale),
        out_shape=jax.ShapeDtypeStruct((M, D), BF16),
        grid=(batch, nt),
        in_specs=[pl.BlockSpec((tq, D), lambda b, t: (b * nt + t, 0)),
                  pl.BlockSpec((mem_len, D), lambda b, t: (b, 0)),
                  pl.BlockSpec((mem_len, D), lambda b, t: (b, 0))],
        out_specs=pl.BlockSpec((tq, D), lambda b, t: (b * nt + t, 0)),
        compiler_params=_params("parallel", "arbitrary"),
    )(q, k, v)


def _pad_cols(w, n):
    return jnp.pad(w, ((0, 0), (0, n - w.shape[1])))


def _pad_rows(w, n):
    return jnp.pad(w, ((0, n - w.shape[0]), (0, 0)))


def kernel(x, mem, w_in, rwkv_shift_mix, rwkv_w0, rwkv_w_up, rwkv_a0, rwkv_a_up, rwkv_g_up, rwkv_k_k, rwkv_k_a, rwkv_r_k, rwkv_gn_g, rwkv_gn_b, conv_w, conv_b, conv_ln_g, conv_ln_b, proj_rwkv, proj_conv, w_out, ln1_g, ln1_b, ln_mem_g, ln_mem_b, xattn_wq, xattn_wk, xattn_wv, xattn_wo, ln2_g, ln2_b, mlp_w1, mlp_w2, ln3_g, ln3_b):
    B, S, D = x.shape
    depth = w_in.shape[0]
    mem_len = mem.shape[1]
    d_rwkv = rwkv_w0.shape[1]
    d_conv = conv_b.shape[1]
    r_decay = rwkv_w_up.shape[1]
    r_iclr = rwkv_a_up.shape[1]
    r_gate = rwkv_g_up.shape[1]
    width = conv_w.shape[1]
    assert r_decay <= LANES and r_iclr <= LANES and r_gate == 2 * LANES
    assert d_rwkv % PAIR == 0 and width - 1 <= CONV_HALO
    alpha = float((2 * depth) ** 0.25)
    M = B * S
    row = lambda a: a.reshape(1, -1)

    mem_n = _ln_rows(mem.reshape(B * mem_len, D), row(ln_mem_g), row(ln_mem_b), BF16)
    h = x.reshape(M, D)
    hb = h.astype(BF16)
    for l in range(depth):
        w = w_in[l]
        c_lora = 3 * d_rwkv
        c_conv = c_lora + r_decay + r_iclr + r_gate
        c_gate = c_conv + 2 * d_conv
        w_rwkv = jnp.concatenate([
            w[:, :c_lora],
            _pad_cols(w[:, c_lora:c_lora + r_decay], LANES),
            _pad_cols(w[:, c_lora + r_decay:c_lora + r_decay + r_iclr], LANES),
            w[:, c_lora + r_decay + r_iclr:c_conv]], axis=1).astype(BF16)
        mix = rwkv_shift_mix[l]
        mu = jnp.concatenate([
            mix[:c_lora],
            jnp.pad(mix[c_lora:c_lora + r_decay], (0, LANES - r_decay)),
            jnp.pad(mix[c_lora + r_decay:c_lora + r_decay + r_iclr], (0, LANES - r_iclr)),
            mix[c_lora + r_decay + r_iclr:]]).reshape(1, -1)
        zs = _mm_shift(hb, w_rwkv, mu, S)
        lora = _lora_act(zs, c_lora // (4 * LANES))
        prm = jnp.stack([rwkv_w0[l], rwkv_a0[l], rwkv_k_k[l], rwkv_k_a[l], rwkv_r_k[l].reshape(-1),
                         rwkv_gn_g[l], rwkv_gn_b[l], jnp.zeros((d_rwkv,), F32)])
        o_r = _rwkv_mix(zs, lora, _pad_rows(rwkv_w_up[l], LANES), _pad_rows(rwkv_a_up[l], LANES),
                        rwkv_g_up[l], prm, B, S, d_rwkv)

        u = _mm_glu(hb, w[:, c_conv:c_conv + d_conv].astype(BF16),
                    w[:, c_conv + d_conv:c_gate].astype(BF16))
        cw = _pad_rows(conv_w[l].reshape(width, d_conv), CONV_HALO)
        o_c = _conv_module(u, cw, row(conv_b[l]), row(conv_ln_g[l]), row(conv_ln_b[l]), B, S, width)

        merged = _mm_merge(hb, o_r, o_c, w[:, c_gate:c_gate + D].astype(BF16),
                           w[:, c_gate + D:].astype(BF16),
                           proj_rwkv[l].astype(BF16), proj_conv[l].astype(BF16))
        h, hb = _mm_res_ln(merged, w_out[l].astype(BF16), h, row(ln1_g[l]), row(ln1_b[l]), alpha)

        q = _mm_plain(hb, xattn_wq[l].astype(BF16), BF16)
        kx = _mm_plain(mem_n, xattn_wk[l].astype(BF16), BF16)
        vx = _mm_plain(mem_n, xattn_wv[l].astype(BF16), BF16)
        ctx = _attention(q, kx, vx, B, S, mem_len, XATTN_HEADS)
        h, hb = _mm_res_ln(ctx, xattn_wo[l].astype(BF16), h, row(ln2_g[l]), row(ln2_b[l]), alpha)

        h = _mlp_ln(hb, mlp_w1[l].astype(BF16), mlp_w2[l].astype(BF16), h,
                    row(ln3_g[l]), row(ln3_b[l]), alpha)
        hb = h.astype(BF16)
    return h.reshape(B, S, D)
```

```python
import functools

import jax
import jax.numpy as jnp
from jax import lax
from jax.experimental import pallas as pl
from jax.experimental.pallas import tpu as pltpu

F32 = jnp.float32
BF16 = jnp.bfloat16

LANES = 128
RWKV_HEAD = 64
PAIR = 2 * RWKV_HEAD
CHUNK = 64
CONV_HALO = 32
XATTN_HEADS = 4
LN_EPS = 1e-5
GN_EPS = 64e-5
VMEM_LIMIT = 56 * 1024 * 1024

_NN = (((1,), (0,)), ((), ()))
_NT = (((1,), (1,)), ((), ()))
_TN = (((0,), (0,)), ((), ()))


def _params(*sem):
    return pltpu.CompilerParams(dimension_semantics=sem, vmem_limit_bytes=VMEM_LIMIT)


def _dot(a, b, dims=_NN):
    return lax.dot_general(a, b, dims, preferred_element_type=F32)


def _split2(x):
    hi = x.astype(BF16)
    lo = (x - hi.astype(F32)).astype(BF16)
    return hi, lo


def _dot3(a, b, dims=_NN):
    ah, al = _split2(a)
    bh, bl = _split2(b)
    return _dot(ah, bh, dims) + (_dot(ah, bl, dims) + _dot(al, bh, dims))


def _dot_exact_rhs(a, b_bf16):
    hi = a.astype(BF16)
    r1 = a - hi.astype(F32)
    mid = r1.astype(BF16)
    lo = (r1 - mid.astype(F32)).astype(BF16)
    return _dot(hi, b_bf16) + (_dot(mid, b_bf16) + _dot(lo, b_bf16))


def _dot_exact_lhs(a_bf16, b):
    hi = b.astype(BF16)
    r1 = b - hi.astype(F32)
    mid = r1.astype(BF16)
    lo = (r1 - mid.astype(F32)).astype(BF16)
    return _dot(a_bf16, hi) + (_dot(a_bf16, mid) + _dot(a_bf16, lo))


def _sigmoid(x):
    return 1.0 / (1.0 + jnp.exp(-x))


def _layer_norm_rows(v, g, b):
    mu = jnp.mean(v, axis=-1, keepdims=True)
    d = v - mu
    var = jnp.mean(d * d, axis=-1, keepdims=True)
    return d * lax.rsqrt(var + LN_EPS) * g + b


def _tile(n, pref):
    t = min(n, pref)
    assert n % t == 0, (n, pref)
    return t


def _mm_shift_kernel(x_ref, w_ref, mu_ref, o_ref, carry_ref, *, tiles_per_seq):
    m = pl.program_id(1)
    z = _dot(x_ref[...], w_ref[...])
    tm = z.shape[0]

    @pl.when(m % tiles_per_seq == 0)
    def _():
        carry_ref[...] = jnp.zeros_like(carry_ref)

    prev_row = carry_ref[0:1, :]
    row = lax.broadcasted_iota(jnp.int32, z.shape, 0)
    prev = jnp.where(row == 0, prev_row, pltpu.roll(z, 1, axis=0))
    o_ref[...] = z + (prev - z) * mu_ref[...]
    carry_ref[0:1, :] = z[tm - 1:tm, :]


def _mm_shift(x, w, mu, seq, tm=1024, tn=512):
    M, K = x.shape
    N = w.shape[1]
    tm = _tile(seq, tm)
    tn = _tile(N, tn)
    return pl.pallas_call(
        functools.partial(_mm_shift_kernel, tiles_per_seq=seq // tm),
        out_shape=jax.ShapeDtypeStruct((M, N), F32),
        grid=(N // tn, M // tm),
        in_specs=[
            pl.BlockSpec((tm, K), lambda n, m: (m, 0)),
            pl.BlockSpec((K, tn), lambda n, m: (0, n)),
            pl.BlockSpec((1, tn), lambda n, m: (0, n)),
        ],
        out_specs=pl.BlockSpec((tm, tn), lambda n, m: (m, n)),
        scratch_shapes=[pltpu.VMEM((8, tn), F32)],
        compiler_params=_params("parallel", "arbitrary"),
    )(x, w, mu)


def _mm_glu_kernel(x_ref, wa_ref, wb_ref, o_ref):
    x = x_ref[...]
    o_ref[...] = _dot(x, wa_ref[...]) * _sigmoid(_dot(x, wb_ref[...]))


def _mm_glu(x, wa, wb, tm=1024, tn=256):
    M, K = x.shape
    N = wa.shape[1]
    tm = _tile(M, tm)
    tn = _tile(N, tn)
    return pl.pallas_call(
        _mm_glu_kernel,
        out_shape=jax.ShapeDtypeStruct((M, N), F32),
        grid=(N // tn, M // tm),
        in_specs=[
            pl.BlockSpec((tm, K), lambda n, m: (m, 0)),
            pl.BlockSpec((K, tn), lambda n, m: (0, n)),
            pl.BlockSpec((K, tn), lambda n, m: (0, n)),
        ],
        out_specs=pl.BlockSpec((tm, tn), lambda n, m: (m, n)),
        compiler_params=_params("parallel", "arbitrary"),
    )(x, wa, wb)


def _mm_plain_kernel(x_ref, w_ref, o_ref):
    o_ref[...] = _dot(x_ref[...], w_ref[...]).astype(o_ref.dtype)


def _mm_plain(x, w, out_dtype, tm=1024, tn=512):
    M, K = x.shape
    N = w.shape[1]
    tm = _tile(M, tm)
    tn = _tile(N, tn)
    return pl.pallas_call(
        _mm_plain_kernel,
        out_shape=jax.ShapeDtypeStruct((M, N), out_dtype),
        grid=(N // tn, M // tm),
        in_specs=[
            pl.BlockSpec((tm, K), lambda n, m: (m, 0)),
            pl.BlockSpec((K, tn), lambda n, m: (0, n)),
        ],
        out_specs=pl.BlockSpec((tm, tn), lambda n, m: (m, n)),
        compiler_params=_params("parallel", "arbitrary"),
    )(x, w)


def _mm_merge_kernel(x_ref, orw_ref, ocv_ref, wgr_ref, wgc_ref, pr_ref, pc_ref, o_ref):
    x = x_ref[...]
    gate_r = _sigmoid(_dot(x, wgr_ref[...]))
    gate_c = _sigmoid(_dot(x, wgc_ref[...]))
    merged = gate_r * _dot(orw_ref[...], pr_ref[...]) + gate_c * _dot(ocv_ref[...], pc_ref[...])
    o_ref[...] = merged.astype(o_ref.dtype)


def _mm_merge(x, o_r, o_c, wgr, wgc, p_r, p_c, tm=512, tn=512):
    M, K = x.shape
    Kb = o_r.shape[1]
    N = wgr.shape[1]
    tm = _tile(M, tm)
    tn = _tile(N, tn)
    row = lambda k: pl.BlockSpec((tm, k), lambda n, m: (m, 0))
    col = lambda k: pl.BlockSpec((k, tn), lambda n, m: (0, n))
    return pl.pallas_call(
        _mm_merge_kernel,
        out_shape=jax.ShapeDtypeStruct((M, N), BF16),
        grid=(N // tn, M // tm),
        in_specs=[row(K), row(Kb), row(Kb), col(K), col(K), col(Kb), col(Kb)],
        out_specs=pl.BlockSpec((tm, tn), lambda n, m: (m, n)),
        compiler_params=_params("parallel", "arbitrary"),
    )(x, o_r, o_c, wgr, wgc, p_r, p_c)


def _ln_inplace(o_ref, ob_ref, g_ref, b_ref, rows=32):
    g = g_ref[...]
    b = b_ref[...]

    def body(i, _):
        sl = pl.ds(pl.multiple_of(i * rows, rows), rows)
        y = _layer_norm_rows(o_ref[sl, :], g, b)
        o_ref[sl, :] = y
        if ob_ref is not None:
            ob_ref[sl, :] = y.astype(ob_ref.dtype)
        return 0

    lax.fori_loop(0, o_ref.shape[0] // rows, body, 0)


def _mm_res_ln_kernel(y_ref, w_ref, res_ref, g_ref, b_ref, o_ref, ob_ref, *, alpha, tn):
    n = pl.program_id(1)
    col = pl.ds(pl.multiple_of(n * tn, tn), tn)
    o_ref[:, col] = alpha * res_ref[...] + _dot(y_ref[...], w_ref[...])

    @pl.when(n == pl.num_programs(1) - 1)
    def _():
        _ln_inplace(o_ref, ob_ref, g_ref, b_ref)


def _mm_res_ln(y, w, res, g, b, alpha, tm=512, tn=512):
    M, K = y.shape
    N = w.shape[1]
    tm = _tile(M, tm)
    tn = _tile(N, tn)
    return pl.pallas_call(
        functools.partial(_mm_res_ln_kernel, alpha=alpha, tn=tn),
        out_shape=(jax.ShapeDtypeStruct((M, N), F32), jax.ShapeDtypeStruct((M, N), BF16)),
        grid=(M // tm, N // tn),
        in_specs=[
            pl.BlockSpec((tm, K), lambda m, n: (m, 0)),
            pl.BlockSpec((K, tn), lambda m, n: (0, n)),
            pl.BlockSpec((tm, tn), lambda m, n: (m, n)),
            pl.BlockSpec((1, N), lambda m, n: (0, 0)),
            pl.BlockSpec((1, N), lambda m, n: (0, 0)),
        ],
        out_specs=(pl.BlockSpec((tm, N), lambda m, n: (m, 0)),
                   pl.BlockSpec((tm, N), lambda m, n: (m, 0))),
        compiler_params=_params("parallel", "arbitrary"),
    )(y, w, res, g, b)


def _mlp_ln_kernel(x_ref, w1_ref, w2_ref, res_ref, g_ref, b_ref, o_ref, *, alpha, tn):
    k = pl.program_id(1)
    h = jnp.maximum(_dot(x_ref[...], w1_ref[...]), 0.0)
    h = (h * h).astype(BF16)
    n_slabs = o_ref.shape[1] // tn

    @pl.when(k == 0)
    def _():
        for j in range(n_slabs):
            sl = slice(j * tn, (j + 1) * tn)
            o_ref[:, sl] = alpha * res_ref[:, sl] + _dot(h, w2_ref[:, sl])

    @pl.when(k > 0)
    def _():
        for j in range(n_slabs):
            sl = slice(j * tn, (j + 1) * tn)
            o_ref[:, sl] += _dot(h, w2_ref[:, sl])

    @pl.when(k == pl.num_programs(1) - 1)
    def _():
        _ln_inplace(o_ref, None, g_ref, b_ref)


def _mlp_ln(x, w1, w2, res, g, b, alpha, tm=512, tk=512, tn=512):
    M, D = x.shape
    FF = w1.shape[1]
    tm = _tile(M, tm)
    tk = _tile(FF, tk)
    tn = _tile(D, tn)
    single = pl.Buffered(1)
    return pl.pallas_call(
        functools.partial(_mlp_ln_kernel, alpha=alpha, tn=tn),
        out_shape=jax.ShapeDtypeStruct((M, D), F32),
        grid=(M // tm, FF // tk),
        in_specs=[
            pl.BlockSpec((tm, D), lambda m, k: (m, 0), pipeline_mode=single),
            pl.BlockSpec((D, tk), lambda m, k: (0, k)),
            pl.BlockSpec((tk, D), lambda m, k: (k, 0)),
            pl.BlockSpec((tm, D), lambda m, k: (m, 0), pipeline_mode=single),
            pl.BlockSpec((1, D), lambda m, k: (0, 0)),
            pl.BlockSpec((1, D), lambda m, k: (0, 0)),
        ],
        out_specs=pl.BlockSpec((tm, D), lambda m, k: (m, 0)),
        compiler_params=_params("parallel", "arbitrary"),
    )(x, w1, w2, res, g, b)


def _ln_rows_kernel(x_ref, g_ref, b_ref, o_ref):
    o_ref[...] = _layer_norm_rows(x_ref[...], g_ref[...], b_ref[...]).astype(o_ref.dtype)


def _ln_rows(x, g, b, out_dtype, tm=128):
    M, D = x.shape
    tm = _tile(M, tm)
    return pl.pallas_call(
        _ln_rows_kernel,
        out_shape=jax.ShapeDtypeStruct((M, D), out_dtype),
        grid=(M // tm,),
        in_specs=[pl.BlockSpec((tm, D), lambda m: (m, 0)),
                  pl.BlockSpec((1, D), lambda m: (0, 0)),
                  pl.BlockSpec((1, D), lambda m: (0, 0))],
        out_specs=pl.BlockSpec((tm, D), lambda m: (m, 0)),
        compiler_params=_params("parallel"),
    )(x, g, b)


def _lora_act_kernel(z_ref, o_ref):
    o_ref[:, 0:LANES] = jnp.tanh(z_ref[:, 0:LANES])
    o_ref[:, LANES:2 * LANES] = z_ref[:, LANES:2 * LANES]
    o_ref[:, 2 * LANES:] = _sigmoid(z_ref[:, 2 * LANES:])


def _lora_act(zs, col_block, tm=1024):
    M = zs.shape[0]
    W = 4 * LANES
    tm = _tile(M, tm)
    return pl.pallas_call(
        _lora_act_kernel,
        out_shape=jax.ShapeDtypeStruct((M, W), F32),
        grid=(M // tm,),
        in_specs=[pl.BlockSpec((tm, W), lambda m: (m, col_block))],
        out_specs=pl.BlockSpec((tm, W), lambda m: (m, 0)),
        compiler_params=_params("parallel"),
    )(zs)


def _rwkv_kernel(zr_ref, zk_ref, zv_ref, l_ref, wup_ref, aup_ref, gup_ref, prm_ref, o_ref, h_ref,
                 *, n_chunks):
    @pl.when(pl.program_id(2) == 0)
    def _():
        h_ref[...] = jnp.zeros_like(h_ref)

    C = CHUNK
    lane = lax.broadcasted_iota(jnp.int32, (1, PAIR), 1)
    m0 = (lane < RWKV_HEAD).astype(F32)
    m1 = 1.0 - m0
    ri = lax.broadcasted_iota(jnp.int32, (PAIR, PAIR), 0)
    ci = lax.broadcasted_iota(jnp.int32, (PAIR, PAIR), 1)
    strict = ci < ri
    lower = ci <= ri
    head_bits = RWKV_HEAD.bit_length() - 1
    same_head = jnp.right_shift(ri, head_bits) == jnp.right_shift(ci, head_bits)
    eye = (ri == ci).astype(F32)
    head_ones = same_head.astype(BF16)
    tt = n_chunks * C
    rt = lax.broadcasted_iota(jnp.int32, (tt, tt), 0)
    ct = lax.broadcasted_iota(jnp.int32, (tt, tt), 1)
    chunk_bits = C.bit_length() - 1
    same_chunk = jnp.right_shift(rt, chunk_bits) == jnp.right_shift(ct, chunk_bits)
    chunk_ones = same_chunk.astype(BF16)
    tril_t = (same_chunk & (ct <= rt)).astype(BF16)

    prm = prm_ref[...]
    w0, a0, k_k, k_a, r_k, gn_g, gn_b = (prm[i:i + 1, :] for i in range(7))
    w_up = wup_ref[...].astype(BF16)
    a_up = aup_ref[...].astype(BF16)
    g_up = gup_ref[...].astype(BF16)

    def stack(x):
        return jnp.concatenate([x * m0, x * m1], axis=0)

    def head_sum(x):
        return _dot_exact_rhs(x, head_ones)

    def bdot(a, b, dims=_NN):
        return _dot(a.astype(BF16), b.astype(BF16), dims)

    r = zr_ref[...]
    k = zk_ref[...]
    v = zv_ref[...]
    lora = l_ref[...].astype(BF16)
    w_raw = w0 + _dot(lora[:, 0:LANES], w_up)
    a_raw = a0 + _dot(lora[:, LANES:2 * LANES], a_up)
    g = _dot(lora[:, 2 * LANES:], g_up)
    w = -(jnp.maximum(-w_raw, 0.0) + jnp.log(1.0 + jnp.exp(-jnp.abs(w_raw)))) - 0.5
    wl = -jnp.exp(w)
    a_sig = _sigmoid(a_raw)
    kk = k * k_k
    kk = kk / jnp.maximum(jnp.sqrt(head_sum(kk * kk)), 1e-12)
    k2 = k * (1.0 + (a_sig - 1.0) * k_a)
    b_s = kk * a_sig

    cum = _dot_exact_lhs(tril_t, wl)
    cum_end = _dot_exact_lhs(chunk_ones, wl)
    e_neg = jnp.exp(-cum)
    e_tail = jnp.exp(cum_end - cum)
    r_t = r * jnp.exp(cum)
    a_t = -kk * jnp.exp(cum - wl)
    b_t = b_s * e_neg
    k_t = k2 * e_neg
    bw = b_s * e_tail
    kw = k2 * e_tail
    w_end = jnp.exp(cum_end)

    chunks = range(n_chunks)
    rows = [slice(c * C, (c + 1) * C) for c in chunks]
    a_st = [stack(a_t[s]).astype(BF16) for s in rows]
    r_st = [stack(r_t[s]) for s in rows]
    v_st = [stack(v[s]).astype(BF16) for s in rows]
    gram = [bdot(jnp.concatenate([a_st[c], r_st[c].astype(BF16)], axis=0),
                 jnp.concatenate([stack(b_t[rows[c]]), stack(k_t[rows[c]])], axis=0), _NT)
            for c in chunks]
    a_ab = [jnp.where(strict, m[:PAIR, :PAIR], 0.0) for m in gram]
    a_ak = [jnp.where(strict, m[:PAIR, PAIR:], 0.0).astype(BF16) for m in gram]
    g_rb = [jnp.where(lower, m[PAIR:, :PAIR], 0.0).astype(BF16) for m in gram]
    g_rk = [jnp.where(lower, m[PAIR:, PAIR:], 0.0).astype(BF16) for m in gram]

    p = a_ab
    t_inv = [eye + m for m in a_ab]
    for _ in range(max(C.bit_length() - 2, 0)):
        p_b = [m.astype(BF16) for m in p]
        p = [_dot(m, m) for m in p_b]
        t_inv = [t_inv[c] + bdot(t_inv[c], p[c]) for c in chunks]
    t_b = [m.astype(BF16) for m in t_inv]

    t_a = [_dot(t_b[c], a_st[c]).astype(BF16) for c in chunks]
    a_v = [_dot(a_ak[c], v_st[c]).astype(BF16) for c in chunks]
    tv_v = [jnp.concatenate([_dot(t_b[c], a_v[c]).astype(BF16), v_st[c]], axis=0) for c in chunks]
    bw_st = [stack(bw[s]).astype(BF16) for s in rows]
    kw_st = [stack(kw[s]).astype(BF16) for s in rows]
    m_c = [(eye * w_end[rows[c]][0:1, :] + _dot(bw_st[c], t_a[c], _TN)).astype(BF16) for c in chunks]
    n_c = [_dot(jnp.concatenate([bw_st[c], kw_st[c]], axis=0), tv_v[c], _TN) for c in chunks]
    q_c = [(r_st[c] + _dot(g_rb[c], t_a[c])).astype(BF16) for c in chunks]
    z_c = [_dot(jnp.concatenate([g_rb[c], g_rk[c]], axis=1), tv_v[c]) for c in chunks]

    h = h_ref[...]
    outs = []
    for c in chunks:
        h_b = h.astype(BF16)
        o_st = _dot(q_c[c], h_b) + z_c[c]
        h = _dot(m_c[c], h_b) + n_c[c]
        outs.append(o_st[:C, :] + o_st[C:, :])
    h_ref[...] = h
    o = jnp.concatenate(outs, axis=0)

    inv_n = 1.0 / RWKV_HEAD
    mu = head_sum(o) * inv_n
    d = o - mu
    var = head_sum(d * d) * inv_n
    y = d * lax.rsqrt(var + GN_EPS) * gn_g + gn_b
    y = y + head_sum(r * k2 * r_k) * v
    o_ref[...] = (y * g).astype(o_ref.dtype)


def _rwkv_mix(zs, lora, w_up, a_up, g_up, prm, batch, seq, d_rwkv, tt=512):
    M = zs.shape[0]
    n_pairs = d_rwkv // PAIR
    tt = _tile(seq, tt)
    nt = seq // tt
    zspec = lambda off: pl.BlockSpec((tt, PAIR), lambda b, p, t: (b * nt + t, off + p))
    wspec = lambda k: pl.BlockSpec((k, PAIR), lambda b, p, t: (0, p))
    return pl.pallas_call(
        functools.partial(_rwkv_kernel, n_chunks=tt // CHUNK),
        out_shape=jax.ShapeDtypeStruct((M, d_rwkv), BF16),
        grid=(batch, n_pairs, nt),
        in_specs=[
            zspec(0), zspec(n_pairs), zspec(2 * n_pairs),
            pl.BlockSpec((tt, 4 * LANES), lambda b, p, t: (b * nt + t, 0)),
            wspec(LANES), wspec(LANES), wspec(2 * LANES), wspec(8),
        ],
        out_specs=pl.BlockSpec((tt, PAIR), lambda b, p, t: (b * nt + t, p)),
        scratch_shapes=[pltpu.VMEM((PAIR, PAIR), F32)],
        compiler_params=_params("parallel", "parallel", "arbitrary"),
    )(zs, zs, zs, lora, w_up, a_up, g_up, prm)


def _conv_kernel(u_ref, w_ref, cb_ref, g_ref, b_ref, o_ref, buf_ref, acc_ref, *, width):
    tt, ch = u_ref.shape
    t = pl.program_id(1)

    @pl.when(t == 0)
    def _():
        buf_ref[0:CONV_HALO, :] = jnp.zeros((CONV_HALO, ch), F32)

    buf_ref[CONV_HALO:, :] = u_ref[...]
    base = CONV_HALO - (width - 1)

    def slab(s, _):
        cols = pl.ds(pl.multiple_of(s * LANES, LANES), LANES)
        acc = jnp.broadcast_to(cb_ref[:, cols], (tt, LANES))
        for j in range(width):
            acc = acc + w_ref[pl.ds(j, 1), cols] * buf_ref[pl.ds(base + j, tt), cols]
        acc_ref[:, cols] = acc
        return 0

    lax.fori_loop(0, ch // LANES, slab, 0)
    buf_ref[0:CONV_HALO, :] = buf_ref[tt:tt + CONV_HALO, :]

    g = g_ref[...]
    b = b_ref[...]
    rows = 32

    def norm(i, _):
        sl = pl.ds(pl.multiple_of(i * rows, rows), rows)
        y = _layer_norm_rows(acc_ref[sl, :], g, b)
        o_ref[sl, :] = (y * _sigmoid(y)).astype(o_ref.dtype)
        return 0

    lax.fori_loop(0, tt // rows, norm, 0)


def _conv_module(u, conv_w, conv_b, ln_g, ln_b, batch, seq, width, tt=256):
    M, ch = u.shape
    tt = _tile(seq, tt)
    nt = seq // tt
    vec = lambda r: pl.BlockSpec((r, ch), lambda b, t: (0, 0))
    return pl.pallas_call(
        functools.partial(_conv_kernel, width=width),
        out_shape=jax.ShapeDtypeStruct((M, ch), BF16),
        grid=(batch, nt),
        in_specs=[pl.BlockSpec((tt, ch), lambda b, t: (b * nt + t, 0)),
                  vec(conv_w.shape[0]), vec(1), vec(1), vec(1)],
        out_specs=pl.BlockSpec((tt, ch), lambda b, t: (b * nt + t, 0)),
        scratch_shapes=[pltpu.VMEM((tt + CONV_HALO, ch), F32), pltpu.VMEM((tt, ch), F32)],
        compiler_params=_params("parallel", "arbitrary"),
    )(u, conv_w, conv_b, ln_g, ln_b)


def _attn_kernel(q_ref, k_ref, v_ref, o_ref, *, heads, scale):
    hd = q_ref.shape[1] // heads
    for h in range(heads):
        sl = slice(h * hd, (h + 1) * hd)
        s = _dot(q_ref[:, sl], k_ref[:, sl], _NT) * scale
        e = jnp.exp(s - jnp.max(s, axis=-1, keepdims=True))
        p = e / jnp.sum(e, axis=-1, keepdims=True)
        o_ref[:, sl] = _dot(p.astype(BF16), v_ref[:, sl]).astype(o_ref.dtype)


def _attention(q, k, v, batch, seq, mem_len, heads, tq=512):
    M, D = q.shape
    tq = _tile(seq, tq)
    nt = seq // tq
    scale = float((D // heads) ** -0.5)
    return pl.pallas_call(
        functools.partial(_attn_kernel, heads=heads, scale=scale),
        out_shape=jax.ShapeDtypeStruct((M, D), BF16),
        grid=(batch, nt),
        in_specs=[pl.BlockSpec((tq, D), lambda b, t: (b * nt + t, 0)),
                  pl.BlockSpec((mem_len, D), lambda b, t: (b, 0)),
                  pl.BlockSpec((mem_len, D), lambda b, t: (b, 0))],
        out_specs=pl.BlockSpec((tq, D), lambda b, t: (b * nt + t, 0)),
        compiler_params=_params("parallel", "arbitrary"),
    )(q, k, v)


def _pad_cols(w, n):
    return jnp.pad(w, ((0, 0), (0, n - w.shape[1])))


def _pad_rows(w, n):
    return jnp.pad(w, ((0, n - w.shape[0]), (0, 0)))


def kernel(x, mem, w_in, rwkv_shift_mix, rwkv_w0, rwkv_w_up, rwkv_a0, rwkv_a_up, rwkv_g_up, rwkv_k_k, rwkv_k_a, rwkv_r_k, rwkv_gn_g, rwkv_gn_b, conv_w, conv_b, conv_ln_g, conv_ln_b, proj_rwkv, proj_conv, w_out, ln1_g, ln1_b, ln_mem_g, ln_mem_b, xattn_wq, xattn_wk, xattn_wv, xattn_wo, ln2_g, ln2_b, mlp_w1, mlp_w2, ln3_g, ln3_b):
    B, S, D = x.shape
    depth = w_in.shape[0]
    mem_len = mem.shape[1]
    d_rwkv = rwkv_w0.shape[1]
    d_conv = conv_b.shape[1]
    r_decay = rwkv_w_up.shape[1]
    r_iclr = rwkv_a_up.shape[1]
    r_gate = rwkv_g_up.shape[1]
    width = conv_w.shape[1]
    assert r_decay <= LANES and r_iclr <= LANES and r_gate == 2 * LANES
    assert d_rwkv % PAIR == 0 and width - 1 <= CONV_HALO
    alpha = float((2 * depth) ** 0.25)
    M = B * S
    row = lambda a: a.reshape(1, -1)

    mem_n = _ln_rows(mem.reshape(B * mem_len, D), row(ln_mem_g), row(ln_mem_b), BF16)
    h = x.reshape(M, D)
    hb = h.astype(BF16)
    for l in range(depth):
        w = w_in[l]
        c_lora = 3 * d_rwkv
        c_conv = c_lora + r_decay + r_iclr + r_gate
        c_gate = c_conv + 2 * d_conv
        w_rwkv = jnp.concatenate([
            w[:, :c_lora],
            _pad_cols(w[:, c_lora:c_lora + r_decay], LANES),
            _pad_cols(w[:, c_lora + r_decay:c_lora + r_decay + r_iclr], LANES),
            w[:, c_lora + r_decay + r_iclr:c_conv]], axis=1).astype(BF16)
        mix = rwkv_shift_mix[l]
        mu = jnp.concatenate([
            mix[:c_lora],
            jnp.pad(mix[c_lora:c_lora + r_decay], (0, LANES - r_decay)),
            jnp.pad(mix[c_lora + r_decay:c_lora + r_decay + r_iclr], (0, LANES - r_iclr)),
            mix[c_lora + r_decay + r_iclr:]]).reshape(1, -1)
        zs = _mm_shift(hb, w_rwkv, mu, S)
        lora = _lora_act(zs, c_lora // (4 * LANES))
        prm = jnp.stack([rwkv_w0[l], rwkv_a0[l], rwkv_k_k[l], rwkv_k_a[l], rwkv_r_k[l].reshape(-1),
                         rwkv_gn_g[l], rwkv_gn_b[l], jnp.zeros((d_rwkv,), F32)])
        o_r = _rwkv_mix(zs, lora, _pad_rows(rwkv_w_up[l], LANES), _pad_rows(rwkv_a_up[l], LANES),
                        rwkv_g_up[l], prm, B, S, d_rwkv)

        u = _mm_glu(hb, w[:, c_conv:c_conv + d_conv].astype(BF16),
                    w[:, c_conv + d_conv:c_gate].astype(BF16))
        cw = _pad_rows(conv_w[l].reshape(width, d_conv), CONV_HALO)
        o_c = _conv_module(u, cw, row(conv_b[l]), row(conv_ln_g[l]), row(conv_ln_b[l]), B, S, width)

        merged = _mm_merge(hb, o_r, o_c, w[:, c_gate:c_gate + D].astype(BF16),
                           w[:, c_gate + D:].astype(BF16),
                           proj_rwkv[l].astype(BF16), proj_conv[l].astype(BF16))
        h, hb = _mm_res_ln(merged, w_out[l].astype(BF16), h, row(ln1_g[l]), row(ln1_b[l]), alpha)

        q = _mm_plain(hb, xattn_wq[l].astype(BF16), BF16)
        kx = _mm_plain(mem_n, xattn_wk[l].astype(BF16), BF16)
        vx = _mm_plain(mem_n, xattn_wv[l].astype(BF16), BF16)
        ctx = _attention(q, kx, vx, B, S, mem_len, XATTN_HEADS)
        h, hb = _mm_res_ln(ctx, xattn_wo[l].astype(BF16), h, row(ln2_g[l]), row(ln2_b[l]), alpha)

        h = _mlp_ln(hb, mlp_w1[l].astype(BF16), mlp_w2[l].astype(BF16), h,
                    row(ln3_g[l]), row(ln3_b[l]), alpha)
        hb = h.astype(BF16)
    return h.reshape(B, S, D)
```

```python
import functools

import jax
import jax.numpy as jnp
from jax import lax
from jax.experimental import pallas as pl
from jax.experimental.pallas import tpu as pltpu

F32 = jnp.float32
BF16 = jnp.bfloat16

LANES = 128
RWKV_HEAD = 64
PAIR = 2 * RWKV_HEAD
CHUNK = 64
CONV_HALO = 32
XATTN_HEADS = 4
LN_EPS = 1e-5
GN_EPS = 64e-5
VMEM_LIMIT = 56 * 1024 * 1024

_NN = (((1,), (0,)), ((), ()))
_NT = (((1,), (1,)), ((), ()))
_TN = (((0,), (0,)), ((), ()))


def _params(*sem):
    return pltpu.CompilerParams(dimension_semantics=sem, vmem_limit_bytes=VMEM_LIMIT)


def _dot(a, b, dims=_NN):
    return lax.dot_general(a, b, dims, preferred_element_type=F32)


def _split2(x):
    hi = x.astype(BF16)
    lo = (x - hi.astype(F32)).astype(BF16)
    return hi, lo


def _dot3(a, b, dims=_NN):
    ah, al = _split2(a)
    bh, bl = _split2(b)
    return _dot(ah, bh, dims) + (_dot(ah, bl, dims) + _dot(al, bh, dims))


def _dot_exact_rhs(a, b_bf16):
    hi = a.astype(BF16)
    r1 = a - hi.astype(F32)
    mid = r1.astype(BF16)
    lo = (r1 - mid.astype(F32)).astype(BF16)
    return _dot(hi, b_bf16) + (_dot(mid, b_bf16) + _dot(lo, b_bf16))


def _dot_exact_lhs(a_bf16, b):
    hi = b.astype(BF16)
    r1 = b - hi.astype(F32)
    mid = r1.astype(BF16)
    lo = (r1 - mid.astype(F32)).astype(BF16)
    return _dot(a_bf16, hi) + (_dot(a_bf16, mid) + _dot(a_bf16, lo))


def _sigmoid(x):
    return 1.0 / (1.0 + jnp.exp(-x))


def _layer_norm_rows(v, g, b):
    mu = jnp.mean(v, axis=-1, keepdims=True)
    d = v - mu
    var = jnp.mean(d * d, axis=-1, keepdims=True)
    return d * lax.rsqrt(var + LN_EPS) * g + b


def _tile(n, pref):
    t = min(n, pref)
    assert n % t == 0, (n, pref)
    return t


def _mm_shift_kernel(x_ref, w_ref, mu_ref, o_ref, carry_ref, *, tiles_per_seq):
    m = pl.program_id(1)
    z = _dot(x_ref[...], w_ref[...])
    tm = z.shape[0]

    @pl.when(m % tiles_per_seq == 0)
    def _():
        carry_ref[...] = jnp.zeros_like(carry_ref)

    prev_row = carry_ref[0:1, :]
    row = lax.broadcasted_iota(jnp.int32, z.shape, 0)
    prev = jnp.where(row == 0, prev_row, pltpu.roll(z, 1, axis=0))
    o_ref[...] = z + (prev - z) * mu_ref[...]
    carry_ref[0:1, :] = z[tm - 1:tm, :]


def _mm_shift(x, w, mu, seq, tm=1024, tn=512):
    M, K = x.shape
    N = w.shape[1]
    tm = _tile(seq, tm)
    tn = _tile(N, tn)
    return pl.pallas_call(
        functools.partial(_mm_shift_kernel, tiles_per_seq=seq // tm),
        out_shape=jax.ShapeDtypeStruct((M, N), F32),
        grid=(N // tn, M // tm),
        in_specs=[
            pl.BlockSpec((tm, K), lambda n, m: (m, 0)),
            pl.BlockSpec((K, tn), lambda n, m: (0, n)),
            pl.BlockSpec((1, tn), lambda n, m: (0, n)),
        ],
        out_specs=pl.BlockSpec((tm, tn), lambda n, m: (m, n)),
        scratch_shapes=[pltpu.VMEM((8, tn), F32)],
        compiler_params=_params("parallel", "arbitrary"),
    )(x, w, mu)


def _mm_glu_kernel(x_ref, wa_ref, wb_ref, o_ref):
    x = x_ref[...]
    o_ref[...] = _dot(x, wa_ref[...]) * _sigmoid(_dot(x, wb_ref[...]))


def _mm_glu(x, wa, wb, tm=1024, tn=256):
    M, K = x.shape
    N = wa.shape[1]
    tm = _tile(M, tm)
    tn = _tile(N, tn)
    return pl.pallas_call(
        _mm_glu_kernel,
        out_shape=jax.ShapeDtypeStruct((M, N), F32),
        grid=(N // tn, M // tm),
        in_specs=[
            pl.BlockSpec((tm, K), lambda n, m: (m, 0)),
            pl.BlockSpec((K, tn), lambda n, m: (0, n)),
            pl.BlockSpec((K, tn), lambda n, m: (0, n)),
        ],
        out_specs=pl.BlockSpec((tm, tn), lambda n, m: (m, n)),
        compiler_params=_params("parallel", "arbitrary"),
    )(x, wa, wb)


def _mm_plain_kernel(x_ref, w_ref, o_ref):
    o_ref[...] = _dot(x_ref[...], w_ref[...]).astype(o_ref.dtype)


def _mm_plain(x, w, out_dtype, tm=1024, tn=512):
    M, K = x.shape
    N = w.shape[1]
    tm = _tile(M, tm)
    tn = _tile(N, tn)
    return pl.pallas_call(
        _mm_plain_kernel,
        out_shape=jax.ShapeDtypeStruct((M, N), out_dtype),
        grid=(N // tn, M // tm),
        in_specs=[
            pl.BlockSpec((tm, K), lambda n, m: (m, 0)),
            pl.BlockSpec((K, tn), lambda n, m: (0, n)),
        ],
        out_specs=pl.BlockSpec((tm, tn), lambda n, m: (m, n)),
        compiler_params=_params("parallel", "arbitrary"),
    )(x, w)


def _mm_merge_kernel(x_ref, orw_ref, ocv_ref, wgr_ref, wgc_ref, pr_ref, pc_ref, o_ref):
    x = x_ref[...]
    gate_r = _sigmoid(_dot(x, wgr_ref[...]))
    gate_c = _sigmoid(_dot(x, wgc_ref[...]))
    merged = gate_r * _dot(orw_ref[...], pr_ref[...]) + gate_c * _dot(ocv_ref[...], pc_ref[...])
    o_ref[...] = merged.astype(o_ref.dtype)


def _mm_merge(x, o_r, o_c, wgr, wgc, p_r, p_c, tm=512, tn=512):
    M, K = x.shape
    Kb = o_r.shape[1]
    N = wgr.shape[1]
    tm = _tile(M, tm)
    tn = _tile(N, tn)
    row = lambda k: pl.BlockSpec((tm, k), lambda n, m: (m, 0))
    col = lambda k: pl.BlockSpec((k, tn), lambda n, m: (0, n))
    return pl.pallas_call(
        _mm_merge_kernel,
        out_shape=jax.ShapeDtypeStruct((M, N), BF16),
        grid=(N // tn, M // tm),
        in_specs=[row(K), row(Kb), row(Kb), col(K), col(K), col(Kb), col(Kb)],
        out_specs=pl.BlockSpec((tm, tn), lambda n, m: (m, n)),
        compiler_params=_params("parallel", "arbitrary"),
    )(x, o_r, o_c, wgr, wgc, p_r, p_c)


def _ln_inplace(o_ref, ob_ref, g_ref, b_ref, rows=32):
    g = g_ref[...]
    b = b_ref[...]

    def body(i, _):
        sl = pl.ds(pl.multiple_of(i * rows, rows), rows)
        y = _layer_norm_rows(o_ref[sl, :], g, b)
        o_ref[sl, :] = y
        if ob_ref is not None:
            ob_ref[sl, :] = y.astype(ob_ref.dtype)
        return 0

    lax.fori_loop(0, o_ref.shape[0] // rows, body, 0)


def _mm_res_ln_kernel(y_ref, w_ref, res_ref, g_ref, b_ref, o_ref, ob_ref, *, alpha, tn):
    n = pl.program_id(1)
    col = pl.ds(pl.multiple_of(n * tn, tn), tn)
    o_ref[:, col] = alpha * res_ref[...] + _dot(y_ref[...], w_ref[...])

    @pl.when(n == pl.num_programs(1) - 1)
    def _():
        _ln_inplace(o_ref, ob_ref, g_ref, b_ref)


def _col_tiles(w, tn):
    K, N = w.shape
    return w.astype(BF16).reshape(K, N // tn, tn).transpose(1, 0, 2)


def _mm_res_ln(y, w, res, g, b, alpha, tm=512, tn=512):
    M, K = y.shape
    N = w.shape[1]
    tm = _tile(M, tm)
    tn = _tile(N, tn)
    return pl.pallas_call(
        functools.partial(_mm_res_ln_kernel, alpha=alpha, tn=tn),
        out_shape=(jax.ShapeDtypeStruct((M, N), F32), jax.ShapeDtypeStruct((M, N), BF16)),
        grid=(M // tm, N // tn),
        in_specs=[
            pl.BlockSpec((tm, K), lambda m, n: (m, 0)),
            pl.BlockSpec((None, K, tn), lambda m, n: (n, 0, 0)),
            pl.BlockSpec((tm, tn), lambda m, n: (m, n)),
            pl.BlockSpec((1, N), lambda m, n: (0, 0)),
            pl.BlockSpec((1, N), lambda m, n: (0, 0)),
        ],
        out_specs=(pl.BlockSpec((tm, N), lambda m, n: (m, 0)),
                   pl.BlockSpec((tm, N), lambda m, n: (m, 0))),
        compiler_params=_params("parallel", "arbitrary"),
    )(y, _col_tiles(w, tn), res, g, b)


def _mlp_ln_kernel(x_ref, w1_ref, w2_ref, res_ref, g_ref, b_ref, o_ref, *, alpha, tn):
    k = pl.program_id(1)
    h = jnp.maximum(_dot(x_ref[...], w1_ref[...]), 0.0)
    h = (h * h).astype(BF16)
    n_slabs = o_ref.shape[1] // tn

    @pl.when(k == 0)
    def _():
        for j in range(n_slabs):
            sl = slice(j * tn, (j + 1) * tn)
            o_ref[:, sl] = _dot(h, w2_ref[:, sl])

    @pl.when(k > 0)
    def _():
        for j in range(n_slabs):
            sl = slice(j * tn, (j + 1) * tn)
            o_ref[:, sl] += _dot(h, w2_ref[:, sl])

    @pl.when(k < n_slabs)
    def _():
        col = pl.ds(pl.multiple_of(k * tn, tn), tn)
        o_ref[:, col] += alpha * res_ref[...]

    @pl.when(k == pl.num_programs(1) - 1)
    def _():
        _ln_inplace(o_ref, None, g_ref, b_ref)


def _mlp_ln(x, w1, w2, res, g, b, alpha, tm=1024, tk=512, tn=512):
    M, D = x.shape
    FF = w1.shape[1]
    tm = _tile(M, tm)
    tk = _tile(FF, tk)
    tn = _tile(D, tn)
    n_slabs = D // tn
    assert FF // tk >= n_slabs
    single = pl.Buffered(1)
    return pl.pallas_call(
        functools.partial(_mlp_ln_kernel, alpha=alpha, tn=tn),
        out_shape=jax.ShapeDtypeStruct((M, D), F32),
        grid=(M // tm, FF // tk),
        in_specs=[
            pl.BlockSpec((tm, D), lambda m, k: (m, 0), pipeline_mode=single),
            pl.BlockSpec((None, D, tk), lambda m, k: (k, 0, 0)),
            pl.BlockSpec((tk, D), lambda m, k: (k, 0)),
            pl.BlockSpec((tm, tn), lambda m, k: (m, jnp.minimum(k, n_slabs - 1))),
            pl.BlockSpec((1, D), lambda m, k: (0, 0)),
            pl.BlockSpec((1, D), lambda m, k: (0, 0)),
        ],
        out_specs=pl.BlockSpec((tm, D), lambda m, k: (m, 0), pipeline_mode=single),
        compiler_params=_params("parallel", "arbitrary"),
    )(x, _col_tiles(w1, tk), w2.astype(BF16), res, g, b)


def _ln_rows_kernel(x_ref, g_ref, b_ref, o_ref):
    o_ref[...] = _layer_norm_rows(x_ref[...], g_ref[...], b_ref[...]).astype(o_ref.dtype)


def _ln_rows(x, g, b, out_dtype, tm=128):
    M, D = x.shape
    tm = _tile(M, tm)
    return pl.pallas_call(
        _ln_rows_kernel,
        out_shape=jax.ShapeDtypeStruct((M, D), out_dtype),
        grid=(M // tm,),
        in_specs=[pl.BlockSpec((tm, D), lambda m: (m, 0)),
                  pl.BlockSpec((1, D), lambda m: (0, 0)),
                  pl.BlockSpec((1, D), lambda m: (0, 0))],
        out_specs=pl.BlockSpec((tm, D), lambda m: (m, 0)),
        compiler_params=_params("parallel"),
    )(x, g, b)


def _lora_act_kernel(z_ref, o_ref):
    o_ref[:, 0:LANES] = jnp.tanh(z_ref[:, 0:LANES])
    o_ref[:, LANES:2 * LANES] = z_ref[:, LANES:2 * LANES]
    o_ref[:, 2 * LANES:] = _sigmoid(z_ref[:, 2 * LANES:])


def _lora_act(zs, col_block, tm=1024):
    M = zs.shape[0]
    W = 4 * LANES
    tm = _tile(M, tm)
    return pl.pallas_call(
        _lora_act_kernel,
        out_shape=jax.ShapeDtypeStruct((M, W), F32),
        grid=(M // tm,),
        in_specs=[pl.BlockSpec((tm, W), lambda m: (m, col_block))],
        out_specs=pl.BlockSpec((tm, W), lambda m: (m, 0)),
        compiler_params=_params("parallel"),
    )(zs)


def _rwkv_kernel(zr_ref, zk_ref, zv_ref, l_ref, wup_ref, aup_ref, gup_ref, prm_ref, o_ref, h_ref,
                 *, n_chunks):
    @pl.when(pl.program_id(2) == 0)
    def _():
        h_ref[...] = jnp.zeros_like(h_ref)

    C = CHUNK
    lane = lax.broadcasted_iota(jnp.int32, (1, PAIR), 1)
    m0 = (lane < RWKV_HEAD).astype(F32)
    m1 = 1.0 - m0
    ri = lax.broadcasted_iota(jnp.int32, (PAIR, PAIR), 0)
    ci = lax.broadcasted_iota(jnp.int32, (PAIR, PAIR), 1)
    strict = ci < ri
    lower = ci <= ri
    head_bits = RWKV_HEAD.bit_length() - 1
    same_head = jnp.right_shift(ri, head_bits) == jnp.right_shift(ci, head_bits)
    eye = (ri == ci).astype(F32)
    head_ones = same_head.astype(BF16)
    rc = lax.broadcasted_iota(jnp.int32, (C, C), 0)
    cc = lax.broadcasted_iota(jnp.int32, (C, C), 1)
    tril_c = (cc <= rc).astype(BF16)
    assert n_chunks % 2 == 0

    prm = prm_ref[...]
    w0, a0, k_k, k_a, r_k, gn_g, gn_b = (prm[i:i + 1, :] for i in range(7))
    w_up = wup_ref[...].astype(BF16)
    a_up = aup_ref[...].astype(BF16)
    g_up = gup_ref[...].astype(BF16)

    def stack(x):
        return jnp.concatenate([x * m0, x * m1], axis=0)

    def head_sum(x, passes):
        acc = None
        for _ in range(passes):
            piece = x.astype(BF16)
            part = _dot(piece, head_ones)
            acc = part if acc is None else acc + part
            x = x - piece.astype(F32)
        return acc

    def bdot(a, b, dims=_NN):
        return _dot(a.astype(BF16), b.astype(BF16), dims)

    r = zr_ref[...]
    k = zk_ref[...]
    v = zv_ref[...]
    lora = l_ref[...].astype(BF16)
    w_raw = w0 + _dot(lora[:, 0:LANES], w_up)
    a_raw = a0 + _dot(lora[:, LANES:2 * LANES], a_up)
    g = _dot(lora[:, 2 * LANES:], g_up)
    w = -(jnp.maximum(-w_raw, 0.0) + jnp.log(1.0 + jnp.exp(-jnp.abs(w_raw)))) - 0.5
    wl = -jnp.exp(w)
    a_sig = _sigmoid(a_raw)
    kk = k * k_k
    kk = kk / jnp.maximum(jnp.sqrt(head_sum(kk * kk, 1)), 1e-12)
    k2 = k * (1.0 + (a_sig - 1.0) * k_a)
    b_s = kk * a_sig

    chunks = range(n_chunks)
    rows = [slice(c * C, (c + 1) * C) for c in chunks]
    wl_hi = wl.astype(BF16)
    wl_r = wl - wl_hi.astype(F32)
    wl_mid = wl_r.astype(BF16)
    wl_lo = (wl_r - wl_mid.astype(F32)).astype(BF16)
    wl_parts = jnp.concatenate([wl_hi, wl_mid, wl_lo], axis=1)
    cum_parts = [_dot(tril_c, wl_parts[s]) for s in rows]
    cum_c = [m[:, :PAIR] + (m[:, PAIR:2 * PAIR] + m[:, 2 * PAIR:]) for m in cum_parts]
    cum = jnp.concatenate(cum_c, axis=0)
    cum_end = jnp.concatenate([jnp.broadcast_to(m[C - 1:C, :], (C, PAIR)) for m in cum_c], axis=0)
    e_neg = jnp.exp(-cum)
    e_tail = jnp.exp(cum_end - cum)
    r_t = r * jnp.exp(cum)
    a_t = -kk * jnp.exp(cum - wl)
    b_t = b_s * e_neg
    k_t = k2 * e_neg
    bw = b_s * e_tail
    kw = k2 * e_tail
    w_end = jnp.exp(cum_end)

    a_st = [stack(a_t[s]).astype(BF16) for s in rows]
    r_st = [stack(r_t[s]) for s in rows]
    v_st = [stack(v[s]).astype(BF16) for s in rows]
    gram = [bdot(jnp.concatenate([a_st[c], r_st[c].astype(BF16)], axis=0),
                 jnp.concatenate([stack(b_t[rows[c]]), stack(k_t[rows[c]])], axis=0), _NT)
            for c in chunks]
    a_ab = [jnp.where(strict, m[:PAIR, :PAIR], 0.0) for m in gram]
    a_ak = [jnp.where(strict, m[:PAIR, PAIR:], 0.0).astype(BF16) for m in gram]
    g_rb = [jnp.where(lower, m[PAIR:, :PAIR], 0.0).astype(BF16) for m in gram]
    g_rk = [jnp.where(lower, m[PAIR:, PAIR:], 0.0).astype(BF16) for m in gram]

    pw = [m.astype(BF16) for m in a_ab]
    pw = [_dot(m, m).astype(BF16) for m in pw]
    t_inv = [eye + m for m in a_ab]
    n = 2
    while 2 * n < C:
        both = [_dot(pw[c], jnp.concatenate([pw[c], t_inv[c].astype(BF16)], axis=1)) for c in chunks]
        pw = [m[:, :PAIR].astype(BF16) for m in both]
        t_inv = [t_inv[c] + both[c][:, PAIR:] for c in chunks]
        n *= 2
    t_b = [(t_inv[c] + _dot(pw[c], t_inv[c].astype(BF16))).astype(BF16) for c in chunks]

    a_v = [_dot(a_ak[c], v_st[c]).astype(BF16) for c in chunks]
    t_av = [_dot(t_b[c], jnp.concatenate([a_st[c], a_v[c]], axis=1)).astype(BF16)
            for c in chunks]
    zero = jnp.zeros((PAIR, PAIR), BF16)
    rhs2 = [jnp.concatenate([t_av[c], jnp.concatenate([zero, v_st[c]], axis=1)], axis=0)
            for c in chunks]
    bw_st = [stack(bw[s]).astype(BF16) for s in rows]
    kw_st = [stack(kw[s]).astype(BF16) for s in rows]
    mn = [_dot(jnp.concatenate([bw_st[c], kw_st[c]], axis=0), rhs2[c], _TN) for c in chunks]
    qz = [_dot(jnp.concatenate([g_rb[c], g_rk[c]], axis=1), rhs2[c]) for c in chunks]
    m_c = [(eye * w_end[rows[c]][0:1, :] + mn[c][:, :PAIR]).astype(BF16) for c in chunks]
    n_c = [mn[c][:, PAIR:] for c in chunks]
    q_c = [(r_st[c] + qz[c][:, :PAIR]).astype(BF16) for c in chunks]
    z_c = [qz[c][:, PAIR:] for c in chunks]

    pairs = range(n_chunks // 2)
    comp = [_dot(m_c[2 * j + 1], jnp.concatenate([m_c[2 * j], n_c[2 * j].astype(BF16)], axis=1))
            for j in pairs]
    m_2 = [comp[j][:, :PAIR].astype(BF16) for j in pairs]
    n_2 = [comp[j][:, PAIR:] + n_c[2 * j + 1] for j in pairs]
    h = h_ref[...]
    outs = []
    for j in pairs:
        c0 = 2 * j
        c1 = c0 + 1
        h_b = h.astype(BF16)
        from_h = _dot(jnp.concatenate([m_2[j], m_c[c0], q_c[c0]], axis=0), h_b)
        h = from_h[:PAIR, :] + n_2[j]
        h_mid = (from_h[PAIR:2 * PAIR, :] + n_c[c0]).astype(BF16)
        o_0 = from_h[2 * PAIR:, :] + z_c[c0]
        o_1 = _dot(q_c[c1], h_mid) + z_c[c1]
        outs.append(o_0[:C, :] + o_0[C:, :])
        outs.append(o_1[:C, :] + o_1[C:, :])
    h_ref[...] = h
    o = jnp.concatenate(outs, axis=0)

    inv_n = 1.0 / RWKV_HEAD
    mu = head_sum(o, 2) * inv_n
    d = o - mu
    var = head_sum(d * d, 2) * inv_n
    y = d * lax.rsqrt(var + GN_EPS) * gn_g + gn_b
    y = y + head_sum(r * k2 * r_k, 1) * v
    o_ref[...] = (y * g).astype(o_ref.dtype)


def _rwkv_mix(zs, lora, w_up, a_up, g_up, prm, batch, seq, d_rwkv, tt=512):
    M = zs.shape[0]
    n_pairs = d_rwkv // PAIR
    tt = _tile(seq, tt)
    nt = seq // tt
    zspec = lambda off: pl.BlockSpec((tt, PAIR), lambda b, p, t: (b * nt + t, off + p))
    wspec = lambda k: pl.BlockSpec((k, PAIR), lambda b, p, t: (0, p))
    return pl.pallas_call(
        functools.partial(_rwkv_kernel, n_chunks=tt // CHUNK),
        out_shape=jax.ShapeDtypeStruct((M, d_rwkv), BF16),
        grid=(batch, n_pairs, nt),
        in_specs=[
            zspec(0), zspec(n_pairs), zspec(2 * n_pairs),
            pl.BlockSpec((tt, 4 * LANES), lambda b, p, t: (b * nt + t, 0)),
            wspec(LANES), wspec(LANES), wspec(2 * LANES), wspec(8),
        ],
        out_specs=pl.BlockSpec((tt, PAIR), lambda b, p, t: (b * nt + t, p)),
        scratch_shapes=[pltpu.VMEM((PAIR, PAIR), F32)],
        compiler_params=_params("parallel", "parallel", "arbitrary"),
    )(zs, zs, zs, lora, w_up, a_up, g_up, prm)


def _conv_kernel(u_ref, w_ref, cb_ref, g_ref, b_ref, o_ref, buf_ref, acc_ref, *, width):
    tt, ch = u_ref.shape
    t = pl.program_id(1)

    @pl.when(t == 0)
    def _():
        buf_ref[0:CONV_HALO, :] = jnp.zeros((CONV_HALO, ch), F32)

    buf_ref[CONV_HALO:, :] = u_ref[...]
    base = CONV_HALO - (width - 1)

    def slab(s, _):
        cols = pl.ds(pl.multiple_of(s * LANES, LANES), LANES)
        acc = jnp.broadcast_to(cb_ref[:, cols], (tt, LANES))
        for j in range(width):
            acc = acc + w_ref[pl.ds(j, 1), cols] * buf_ref[pl.ds(base + j, tt), cols]
        acc_ref[:, cols] = acc
        return 0

    lax.fori_loop(0, ch // LANES, slab, 0)
    buf_ref[0:CONV_HALO, :] = buf_ref[tt:tt + CONV_HALO, :]

    g = g_ref[...]
    b = b_ref[...]
    rows = 32

    def norm(i, _):
        sl = pl.ds(pl.multiple_of(i * rows, rows), rows)
        y = _layer_norm_rows(acc_ref[sl, :], g, b)
        o_ref[sl, :] = (y * _sigmoid(y)).astype(o_ref.dtype)
        return 0

    lax.fori_loop(0, tt // rows, norm, 0)


def _conv_module(u, conv_w, conv_b, ln_g, ln_b, batch, seq, width, tt=256):
    M, ch = u.shape
    tt = _tile(seq, tt)
    nt = seq // tt
    vec = lambda r: pl.BlockSpec((r, ch), lambda b, t: (0, 0))
    return pl.pallas_call(
        functools.partial(_conv_kernel, width=width),
        out_shape=jax.ShapeDtypeStruct((M, ch), BF16),
        grid=(batch, nt),
        in_specs=[pl.BlockSpec((tt, ch), lambda b, t: (b * nt + t, 0)),
                  vec(conv_w.shape[0]), vec(1), vec(1), vec(1)],
        out_specs=pl.BlockSpec((tt, ch), lambda b, t: (b * nt + t, 0)),
        scratch_shapes=[pltpu.VMEM((tt + CONV_HALO, ch), F32), pltpu.VMEM((tt, ch), F32)],
        compiler_params=_params("parallel", "arbitrary"),
    )(u, conv_w, conv_b, ln_g, ln_b)


def _qk_fold_kernel(wq_ref, k_ref, o_ref):
    o_ref[...] = _dot(wq_ref[...], k_ref[...], _NT).astype(o_ref.dtype)


def _qk_fold(wq, kx, batch, mem_len, heads):
    D = wq.shape[0]
    hd = D // heads
    return pl.pallas_call(
        _qk_fold_kernel,
        out_shape=jax.ShapeDtypeStruct((batch, D, heads * mem_len), BF16),
        grid=(heads, batch),
        in_specs=[pl.BlockSpec((D, hd), lambda h, b: (0, h)),
                  pl.BlockSpec((mem_len, hd), lambda h, b: (b, h))],
        out_specs=pl.BlockSpec((None, D, mem_len), lambda h, b: (b, 0, h)),
        compiler_params=_params("parallel", "arbitrary"),
    )(wq, kx)


def _vo_fold_kernel(v_ref, wo_ref, o_ref):
    o_ref[...] = _dot(v_ref[...], wo_ref[...]).astype(o_ref.dtype)


def _vo_fold(vx, wo, batch, mem_len, heads):
    D = wo.shape[0]
    hd = D // heads
    return pl.pallas_call(
        _vo_fold_kernel,
        out_shape=jax.ShapeDtypeStruct((batch, heads * mem_len, D), BF16),
        grid=(heads, batch),
        in_specs=[pl.BlockSpec((mem_len, hd), lambda h, b: (b, h)),
                  pl.BlockSpec((hd, D), lambda h, b: (h, 0))],
        out_specs=pl.BlockSpec((None, mem_len, D), lambda h, b: (b, h, 0)),
        compiler_params=_params("parallel", "arbitrary"),
    )(vx, wo)


def _attn_ln_kernel(x_ref, qk_ref, vo_ref, res_ref, g_ref, b_ref, o_ref, ob_ref, *, heads, scale, alpha):
    s = _dot(x_ref[...], qk_ref[...]) * scale
    mem_len = s.shape[1] // heads
    probs = []
    for h in range(heads):
        sh = s[:, h * mem_len:(h + 1) * mem_len]
        e = jnp.exp(sh - jnp.max(sh, axis=-1, keepdims=True))
        probs.append((e / jnp.sum(e, axis=-1, keepdims=True)).astype(BF16))
    p = jnp.concatenate(probs, axis=1)
    o_ref[...] = alpha * res_ref[...] + _dot(p, vo_ref[...])
    _ln_inplace(o_ref, ob_ref, g_ref, b_ref)


def _attn_ln(x, qk, vo, res, g, b, alpha, batch, seq, heads, tq=256):
    M, D = x.shape
    hm = qk.shape[2]
    tq = _tile(seq, tq)
    nt = seq // tq
    scale = float((D // heads) ** -0.5)
    single = pl.Buffered(1)
    tok = lambda: pl.BlockSpec((tq, D), lambda bi, t: (bi * nt + t, 0))
    vec = pl.BlockSpec((1, D), lambda bi, t: (0, 0))
    return pl.pallas_call(
        functools.partial(_attn_ln_kernel, heads=heads, scale=scale, alpha=alpha),
        out_shape=(jax.ShapeDtypeStruct((M, D), F32), jax.ShapeDtypeStruct((M, D), BF16)),
        grid=(batch, nt),
        in_specs=[tok(),
                  pl.BlockSpec((None, D, hm), lambda bi, t: (bi, 0, 0), pipeline_mode=single),
                  pl.BlockSpec((None, hm, D), lambda bi, t: (bi, 0, 0), pipeline_mode=single),
                  tok(), vec, vec],
        out_specs=(tok(), tok()),
        compiler_params=_params("parallel", "arbitrary"),
    )(x, qk, vo, res, g, b)


def _pad_cols(w, n):
    return jnp.pad(w, ((0, 0), (0, n - w.shape[1])))


def _pad_rows(w, n):
    return jnp.pad(w, ((0, n - w.shape[0]), (0, 0)))


def kernel(x, mem, w_in, rwkv_shift_mix, rwkv_w0, rwkv_w_up, rwkv_a0, rwkv_a_up, rwkv_g_up, rwkv_k_k, rwkv_k_a, rwkv_r_k, rwkv_gn_g, rwkv_gn_b, conv_w, conv_b, conv_ln_g, conv_ln_b, proj_rwkv, proj_conv, w_out, ln1_g, ln1_b, ln_mem_g, ln_mem_b, xattn_wq, xattn_wk, xattn_wv, xattn_wo, ln2_g, ln2_b, mlp_w1, mlp_w2, ln3_g, ln3_b):
    B, S, D = x.shape
    depth = w_in.shape[0]
    mem_len = mem.shape[1]
    d_rwkv = rwkv_w0.shape[1]
    d_conv = conv_b.shape[1]
    r_decay = rwkv_w_up.shape[1]
    r_iclr = rwkv_a_up.shape[1]
    r_gate = rwkv_g_up.shape[1]
    width = conv_w.shape[1]
    assert r_decay <= LANES and r_iclr <= LANES and r_gate == 2 * LANES
    assert d_rwkv % PAIR == 0 and width - 1 <= CONV_HALO
    alpha = float((2 * depth) ** 0.25)
    M = B * S
    row = lambda a: a.reshape(1, -1)

    mem_n = _ln_rows(mem.reshape(B * mem_len, D), row(ln_mem_g), row(ln_mem_b), BF16)
    h = x.reshape(M, D)
    hb = h.astype(BF16)
    for l in range(depth):
        w = w_in[l]
        c_lora = 3 * d_rwkv
        c_conv = c_lora + r_decay + r_iclr + r_gate
        c_gate = c_conv + 2 * d_conv
        w_rwkv = jnp.concatenate([
            w[:, :c_lora],
            _pad_cols(w[:, c_lora:c_lora + r_decay], LANES),
            _pad_cols(w[:, c_lora + r_decay:c_lora + r_decay + r_iclr], LANES),
            w[:, c_lora + r_decay + r_iclr:c_conv]], axis=1).astype(BF16)
        mix = rwkv_shift_mix[l]
        mu = jnp.concatenate([
            mix[:c_lora],
            jnp.pad(mix[c_lora:c_lora + r_decay], (0, LANES - r_decay)),
            jnp.pad(mix[c_lora + r_decay:c_lora + r_decay + r_iclr], (0, LANES - r_iclr)),
            mix[c_lora + r_decay + r_iclr:]]).reshape(1, -1)
        zs = _mm_shift(hb, w_rwkv, mu, S)
        lora = _lora_act(zs, c_lora // (4 * LANES))
        prm = jnp.stack([rwkv_w0[l], rwkv_a0[l], rwkv_k_k[l], rwkv_k_a[l], rwkv_r_k[l].reshape(-1),
                         rwkv_gn_g[l], rwkv_gn_b[l], jnp.zeros((d_rwkv,), F32)])
        o_r = _rwkv_mix(zs, lora, _pad_rows(rwkv_w_up[l], LANES), _pad_rows(rwkv_a_up[l], LANES),
                        rwkv_g_up[l], prm, B, S, d_rwkv)

        u = _mm_glu(hb, w[:, c_conv:c_conv + d_conv].astype(BF16),
                    w[:, c_conv + d_conv:c_gate].astype(BF16))
        cw = _pad_rows(conv_w[l].reshape(width, d_conv), CONV_HALO)
        o_c = _conv_module(u, cw, row(conv_b[l]), row(conv_ln_g[l]), row(conv_ln_b[l]), B, S, width)

        merged = _mm_merge(hb, o_r, o_c, w[:, c_gate:c_gate + D].astype(BF16),
                           w[:, c_gate + D:].astype(BF16),
                           proj_rwkv[l].astype(BF16), proj_conv[l].astype(BF16))
        h, hb = _mm_res_ln(merged, w_out[l], h, row(ln1_g[l]), row(ln1_b[l]), alpha)

        kx = _mm_plain(mem_n, xattn_wk[l].astype(BF16), BF16)
        vx = _mm_plain(mem_n, xattn_wv[l].astype(BF16), BF16)
        qk = _qk_fold(xattn_wq[l].astype(BF16), kx, B, mem_len, XATTN_HEADS)
        vo = _vo_fold(vx, xattn_wo[l].astype(BF16), B, mem_len, XATTN_HEADS)
        h, hb = _attn_ln(hb, qk, vo, h, row(ln2_g[l]), row(ln2_b[l]), alpha, B, S, XATTN_HEADS)

        h = _mlp_ln(hb, mlp_w1[l], mlp_w2[l], h, row(ln3_g[l]), row(ln3_b[l]), alpha)
        hb = h.astype(BF16)
    return h.reshape(B, S, D)
```

```python
import functools

import jax
import jax.numpy as jnp
from jax import lax
from jax.experimental import pallas as pl
from jax.experimental.pallas import tpu as pltpu

F32 = jnp.float32
BF16 = jnp.bfloat16

LANES = 128
RWKV_HEAD = 64
PAIR = 2 * RWKV_HEAD
CHUNK = 64
CONV_HALO = 32
XATTN_HEADS = 4
LN_EPS = 1e-5
GN_EPS = 64e-5
VMEM_LIMIT = 56 * 1024 * 1024

_NN = (((1,), (0,)), ((), ()))
_NT = (((1,), (1,)), ((), ()))
_TN = (((0,), (0,)), ((), ()))
_DONE = object()


def _params(*sem):
    return pltpu.CompilerParams(dimension_semantics=sem, vmem_limit_bytes=VMEM_LIMIT)


def _dot(a, b, dims=_NN):
    return lax.dot_general(a, b, dims, preferred_element_type=F32)


def _sigmoid(x):
    return 1.0 / (1.0 + jnp.exp(-x))


def _layer_norm_rows(v, g, b):
    mu = jnp.mean(v, axis=-1, keepdims=True)
    d = v - mu
    var = jnp.mean(d * d, axis=-1, keepdims=True)
    return d * lax.rsqrt(var + LN_EPS) * g + b


def _tile(n, pref):
    t = min(n, pref)
    assert n % t == 0, (n, pref)
    return t


def _mm_shift_kernel(x_ref, w_ref, mu_ref, o_ref, carry_ref, *, tiles_per_seq):
    m = pl.program_id(1)
    z = _dot(x_ref[...], w_ref[...])
    tm = z.shape[0]

    @pl.when(m % tiles_per_seq == 0)
    def _():
        carry_ref[...] = jnp.zeros_like(carry_ref)

    prev_row = carry_ref[0:1, :]
    row = lax.broadcasted_iota(jnp.int32, z.shape, 0)
    prev = jnp.where(row == 0, prev_row, pltpu.roll(z, 1, axis=0))
    o_ref[...] = z + (prev - z) * mu_ref[...]
    carry_ref[0:1, :] = z[tm - 1:tm, :]


def _mm_shift(x, w, mu, seq, tm=1024, tn=512):
    M, K = x.shape
    N = w.shape[1]
    tm = _tile(seq, tm)
    tn = _tile(N, tn)
    return pl.pallas_call(
        functools.partial(_mm_shift_kernel, tiles_per_seq=seq // tm),
        out_shape=jax.ShapeDtypeStruct((M, N), F32),
        grid=(N // tn, M // tm),
        in_specs=[
            pl.BlockSpec((tm, K), lambda n, m: (m, 0)),
            pl.BlockSpec((K, tn), lambda n, m: (0, n)),
            pl.BlockSpec((1, tn), lambda n, m: (0, n)),
        ],
        out_specs=pl.BlockSpec((tm, tn), lambda n, m: (m, n)),
        scratch_shapes=[pltpu.VMEM((8, tn), F32)],
        compiler_params=_params("parallel", "arbitrary"),
    )(x, w, mu)


def _mm_glu_kernel(x_ref, wa_ref, wb_ref, o_ref):
    x = x_ref[...]
    o_ref[...] = _dot(x, wa_ref[...]) * _sigmoid(_dot(x, wb_ref[...]))


def _mm_glu(x, wa, wb, tm=1024, tn=256):
    M, K = x.shape
    N = wa.shape[1]
    tm = _tile(M, tm)
    tn = _tile(N, tn)
    return pl.pallas_call(
        _mm_glu_kernel,
        out_shape=jax.ShapeDtypeStruct((M, N), F32),
        grid=(N // tn, M // tm),
        in_specs=[
            pl.BlockSpec((tm, K), lambda n, m: (m, 0)),
            pl.BlockSpec((K, tn), lambda n, m: (0, n)),
            pl.BlockSpec((K, tn), lambda n, m: (0, n)),
        ],
        out_specs=pl.BlockSpec((tm, tn), lambda n, m: (m, n)),
        compiler_params=_params("parallel", "arbitrary"),
    )(x, wa, wb)


def _mm_plain_kernel(x_ref, w_ref, o_ref):
    o_ref[...] = _dot(x_ref[...], w_ref[...]).astype(o_ref.dtype)


def _mm_plain(x, w, out_dtype, tm=1024, tn=512):
    M, K = x.shape
    N = w.shape[1]
    tm = _tile(M, tm)
    tn = _tile(N, tn)
    return pl.pallas_call(
        _mm_plain_kernel,
        out_shape=jax.ShapeDtypeStruct((M, N), out_dtype),
        grid=(N // tn, M // tm),
        in_specs=[
            pl.BlockSpec((tm, K), lambda n, m: (m, 0)),
            pl.BlockSpec((K, tn), lambda n, m: (0, n)),
        ],
        out_specs=pl.BlockSpec((tm, tn), lambda n, m: (m, n)),
        compiler_params=_params("parallel", "arbitrary"),
    )(x, w)


def _mm_merge_kernel(x_ref, orw_ref, ocv_ref, wgr_ref, wgc_ref, pr_ref, pc_ref, o_ref):
    x = x_ref[...]
    gate_r = _sigmoid(_dot(x, wgr_ref[...]))
    gate_c = _sigmoid(_dot(x, wgc_ref[...]))
    merged = gate_r * _dot(orw_ref[...], pr_ref[...]) + gate_c * _dot(ocv_ref[...], pc_ref[...])
    o_ref[...] = merged.astype(o_ref.dtype)


def _mm_merge(x, o_r, o_c, wgr, wgc, p_r, p_c, tm=512, tn=512):
    M, K = x.shape
    Kb = o_r.shape[1]
    N = wgr.shape[1]
    tm = _tile(M, tm)
    tn = _tile(N, tn)
    row = lambda k: pl.BlockSpec((tm, k), lambda n, m: (m, 0))
    col = lambda k: pl.BlockSpec((k, tn), lambda n, m: (0, n))
    return pl.pallas_call(
        _mm_merge_kernel,
        out_shape=jax.ShapeDtypeStruct((M, N), BF16),
        grid=(N // tn, M // tm),
        in_specs=[row(K), row(Kb), row(Kb), col(K), col(K), col(Kb), col(Kb)],
        out_specs=pl.BlockSpec((tm, tn), lambda n, m: (m, n)),
        compiler_params=_params("parallel", "arbitrary"),
    )(x, o_r, o_c, wgr, wgc, p_r, p_c)


def _ln_inplace(o_ref, ob_ref, g_ref, b_ref, rows=128):
    g = g_ref[...]
    b = b_ref[...]

    def body(i, _):
        sl = pl.ds(pl.multiple_of(i * rows, rows), rows)
        y = _layer_norm_rows(o_ref[sl, :], g, b)
        o_ref[sl, :] = y
        if ob_ref is not None:
            ob_ref[sl, :] = y.astype(ob_ref.dtype)
        return 0

    lax.fori_loop(0, o_ref.shape[0] // rows, body, 0)


def _mm_res_ln_kernel(y_ref, w_ref, res_ref, g_ref, b_ref, o_ref, ob_ref, *, alpha, tn):
    n = pl.program_id(1)
    col = pl.ds(pl.multiple_of(n * tn, tn), tn)
    o_ref[:, col] = alpha * res_ref[...] + _dot(y_ref[...], w_ref[...])

    @pl.when(n == pl.num_programs(1) - 1)
    def _():
        _ln_inplace(o_ref, ob_ref, g_ref, b_ref)


def _mm_res_ln(y, w, res, g, b, alpha, tm=512, tn=512):
    M, K = y.shape
    N = w.shape[1]
    tm = _tile(M, tm)
    tn = _tile(N, tn)
    return pl.pallas_call(
        functools.partial(_mm_res_ln_kernel, alpha=alpha, tn=tn),
        out_shape=(jax.ShapeDtypeStruct((M, N), F32), jax.ShapeDtypeStruct((M, N), BF16)),
        grid=(M // tm, N // tn),
        in_specs=[
            pl.BlockSpec((tm, K), lambda m, n: (m, 0)),
            pl.BlockSpec((K, tn), lambda m, n: (0, n)),
            pl.BlockSpec((tm, tn), lambda m, n: (m, n)),
            pl.BlockSpec((1, N), lambda m, n: (0, 0)),
            pl.BlockSpec((1, N), lambda m, n: (0, 0)),
        ],
        out_specs=(pl.BlockSpec((tm, N), lambda m, n: (m, 0)),
                   pl.BlockSpec((tm, N), lambda m, n: (m, 0))),
        compiler_params=_params("parallel", "arbitrary"),
    )(y, w.astype(BF16), res, g, b)


def _mlp_ln_kernel(x_ref, w1_ref, w2_ref, res_ref, g_ref, b_ref, o_ref, *, alpha, tn):
    k = pl.program_id(1)
    h = jnp.maximum(_dot(x_ref[...], w1_ref[...]), 0.0)
    h = (h * h).astype(BF16)
    n_slabs = o_ref.shape[1] // tn

    @pl.when(k == 0)
    def _():
        for j in range(n_slabs):
            sl = slice(j * tn, (j + 1) * tn)
            o_ref[:, sl] = _dot(h, w2_ref[:, sl])

    @pl.when(k > 0)
    def _():
        for j in range(n_slabs):
            sl = slice(j * tn, (j + 1) * tn)
            o_ref[:, sl] += _dot(h, w2_ref[:, sl])

    @pl.when(k < n_slabs)
    def _():
        col = pl.ds(pl.multiple_of(k * tn, tn), tn)
        o_ref[:, col] += alpha * res_ref[...]

    @pl.when(k == pl.num_programs(1) - 1)
    def _():
        _ln_inplace(o_ref, None, g_ref, b_ref)


def _mlp_ln(x, w1, w2, res, g, b, alpha, tm=1024, tk=512, tn=512):
    M, D = x.shape
    FF = w1.shape[1]
    tm = _tile(M, tm)
    tk = _tile(FF, tk)
    tn = _tile(D, tn)
    n_slabs = D // tn
    assert FF // tk >= n_slabs
    single = pl.Buffered(1)
    return pl.pallas_call(
        functools.partial(_mlp_ln_kernel, alpha=alpha, tn=tn),
        out_shape=jax.ShapeDtypeStruct((M, D), F32),
        grid=(M // tm, FF // tk),
        in_specs=[
            pl.BlockSpec((tm, D), lambda m, k: (m, 0), pipeline_mode=single),
            pl.BlockSpec((D, tk), lambda m, k: (0, k)),
            pl.BlockSpec((tk, D), lambda m, k: (k, 0)),
            pl.BlockSpec((tm, tn), lambda m, k: (m, jnp.minimum(k, n_slabs - 1))),
            pl.BlockSpec((1, D), lambda m, k: (0, 0)),
            pl.BlockSpec((1, D), lambda m, k: (0, 0)),
        ],
        out_specs=pl.BlockSpec((tm, D), lambda m, k: (m, 0), pipeline_mode=single),
        compiler_params=_params("parallel", "arbitrary"),
    )(x, w1.astype(BF16), w2.astype(BF16), res, g, b)


def _ln_rows_kernel(x_ref, g_ref, b_ref, o_ref):
    o_ref[...] = _layer_norm_rows(x_ref[...], g_ref[...], b_ref[...]).astype(o_ref.dtype)


def _ln_rows(x, g, b, out_dtype, tm=128):
    M, D = x.shape
    tm = _tile(M, tm)
    return pl.pallas_call(
        _ln_rows_kernel,
        out_shape=jax.ShapeDtypeStruct((M, D), out_dtype),
        grid=(M // tm,),
        in_specs=[pl.BlockSpec((tm, D), lambda m: (m, 0)),
                  pl.BlockSpec((1, D), lambda m: (0, 0)),
                  pl.BlockSpec((1, D), lambda m: (0, 0))],
        out_specs=pl.BlockSpec((tm, D), lambda m: (m, 0)),
        compiler_params=_params("parallel"),
    )(x, g, b)


def _lora_act_kernel(z_ref, o_ref):
    o_ref[:, 0:LANES] = jnp.tanh(z_ref[:, 0:LANES])
    o_ref[:, LANES:2 * LANES] = z_ref[:, LANES:2 * LANES]
    o_ref[:, 2 * LANES:] = _sigmoid(z_ref[:, 2 * LANES:])


def _lora_act(zs, col_block, tm=1024):
    M = zs.shape[0]
    W = 4 * LANES
    tm = _tile(M, tm)
    return pl.pallas_call(
        _lora_act_kernel,
        out_shape=jax.ShapeDtypeStruct((M, W), F32),
        grid=(M // tm,),
        in_specs=[pl.BlockSpec((tm, W), lambda m: (m, col_block))],
        out_specs=pl.BlockSpec((tm, W), lambda m: (m, 0)),
        compiler_params=_params("parallel"),
    )(zs)


def _rwkv_pair_stages(q, zr_ref, zk_ref, zv_ref, l_ref, wup_ref, aup_ref, gup_ref, prm_ref, o_ref, h_ref,
                      n_chunks):
    C = CHUNK
    lane = lax.broadcasted_iota(jnp.int32, (1, PAIR), 1)
    m0 = (lane < RWKV_HEAD).astype(F32)
    m1 = 1.0 - m0
    ri = lax.broadcasted_iota(jnp.int32, (PAIR, PAIR), 0)
    ci = lax.broadcasted_iota(jnp.int32, (PAIR, PAIR), 1)
    strict = ci < ri
    lower = ci <= ri
    head_bits = RWKV_HEAD.bit_length() - 1
    same_head = jnp.right_shift(ri, head_bits) == jnp.right_shift(ci, head_bits)
    eye = (ri == ci).astype(F32)
    head_ones = same_head.astype(BF16)
    rc = lax.broadcasted_iota(jnp.int32, (C, C), 0)
    cc = lax.broadcasted_iota(jnp.int32, (C, C), 1)
    tril_c = (cc <= rc).astype(BF16)
    assert n_chunks % 2 == 0

    lanes = slice(q * PAIR, (q + 1) * PAIR)
    prm = prm_ref[:, lanes]
    w0, a0, k_k, k_a, r_k, gn_g, gn_b = (prm[i:i + 1, :] for i in range(7))
    w_up = wup_ref[:, lanes].astype(BF16)
    a_up = aup_ref[:, lanes].astype(BF16)
    g_up = gup_ref[:, lanes].astype(BF16)

    def stack(x):
        return jnp.concatenate([x * m0, x * m1], axis=0)

    def head_sum(x, passes):
        acc = None
        for _ in range(passes):
            piece = x.astype(BF16)
            part = _dot(piece, head_ones)
            acc = part if acc is None else acc + part
            x = x - piece.astype(F32)
        return acc

    def bdot(a, b, dims=_NN):
        return _dot(a.astype(BF16), b.astype(BF16), dims)

    r = zr_ref[:, lanes]
    k = zk_ref[:, lanes]
    v = zv_ref[:, lanes]
    lora = l_ref[...].astype(BF16)
    w_raw = w0 + _dot(lora[:, 0:LANES], w_up)
    a_raw = a0 + _dot(lora[:, LANES:2 * LANES], a_up)
    g = _dot(lora[:, 2 * LANES:], g_up)
    w = -(jnp.maximum(-w_raw, 0.0) + jnp.log(1.0 + jnp.exp(-jnp.abs(w_raw)))) - 0.5
    wl = -jnp.exp(w)
    a_sig = _sigmoid(a_raw)
    kk = k * k_k
    kk = kk / jnp.maximum(jnp.sqrt(head_sum(kk * kk, 1)), 1e-12)
    k2 = k * (1.0 + (a_sig - 1.0) * k_a)
    b_s = kk * a_sig
    yield

    chunks = range(n_chunks)
    rows = [slice(c * C, (c + 1) * C) for c in chunks]
    wl_hi = wl.astype(BF16)
    wl_r = wl - wl_hi.astype(F32)
    wl_mid = wl_r.astype(BF16)
    wl_lo = (wl_r - wl_mid.astype(F32)).astype(BF16)
    wl_parts = jnp.concatenate([wl_hi, wl_mid, wl_lo], axis=1)
    cum_parts = [_dot(tril_c, wl_parts[s]) for s in rows]
    cum_c = [m[:, :PAIR] + (m[:, PAIR:2 * PAIR] + m[:, 2 * PAIR:]) for m in cum_parts]
    cum = jnp.concatenate(cum_c, axis=0)
    cum_end = jnp.concatenate([jnp.broadcast_to(m[C - 1:C, :], (C, PAIR)) for m in cum_c], axis=0)
    e_neg = jnp.exp(-cum)
    e_tail = jnp.exp(cum_end - cum)
    r_t = r * jnp.exp(cum)
    a_t = -kk * jnp.exp(cum - wl)
    b_t = b_s * e_neg
    k_t = k2 * e_neg
    bw = b_s * e_tail
    kw = k2 * e_tail
    w_end = jnp.exp(cum_end)
    yield

    a_st = [stack(a_t[s]).astype(BF16) for s in rows]
    r_st = [stack(r_t[s]) for s in rows]
    v_st = [stack(v[s]).astype(BF16) for s in rows]
    gram = [bdot(jnp.concatenate([a_st[c], r_st[c].astype(BF16)], axis=0),
                 jnp.concatenate([stack(b_t[rows[c]]), stack(k_t[rows[c]])], axis=0), _NT)
            for c in chunks]
    yield
    a_ab =[jnp.where(strict, m[:PAIR, :PAIR], 0.0) for m in gram]
    a_ak = [jnp.where(strict, m[:PAIR, PAIR:], 0.0).astype(BF16) for m in gram]
    g_rb = [jnp.where(lower, m[PAIR:, :PAIR], 0.0).astype(BF16) for m in gram]
    g_rk = [jnp.where(lower, m[PAIR:, PAIR:], 0.0).astype(BF16) for m in gram]

    pw = [m.astype(BF16) for m in a_ab]
    pw = [_dot(m, m).astype(BF16) for m in pw]
    t_inv = [eye + m for m in a_ab]
    yield
    n = 2
    while 2 * n < C:
        both = [_dot(pw[c], jnp.concatenate([pw[c], t_inv[c].astype(BF16)], axis=1)) for c in chunks]
        pw = [m[:, :PAIR].astype(BF16) for m in both]
        t_inv = [t_inv[c] + both[c][:, PAIR:] for c in chunks]
        n *= 2
        yield
    t_b = [(t_inv[c] + _dot(pw[c], t_inv[c].astype(BF16))).astype(BF16) for c in chunks]

    a_v = [_dot(a_ak[c], v_st[c]).astype(BF16) for c in chunks]
    yield
    t_av =[_dot(t_b[c], jnp.concatenate([a_st[c], a_v[c]], axis=1)).astype(BF16)
            for c in chunks]
    zero = jnp.zeros((PAIR, PAIR), BF16)
    rhs2 = [jnp.concatenate([t_av[c], jnp.concatenate([zero, v_st[c]], axis=1)], axis=0)
            for c in chunks]
    yield
    bw_st = [stack(bw[s]).astype(BF16) for s in rows]
    kw_st = [stack(kw[s]).astype(BF16) for s in rows]
    mn = [_dot(jnp.concatenate([bw_st[c], kw_st[c]], axis=0), rhs2[c], _TN) for c in chunks]
    qz = [_dot(jnp.concatenate([g_rb[c], g_rk[c]], axis=1), rhs2[c]) for c in chunks]
    m_c = [(eye * w_end[rows[c]][0:1, :] + mn[c][:, :PAIR]).astype(BF16) for c in chunks]
    n_c = [mn[c][:, PAIR:] for c in chunks]
    q_c = [(r_st[c] + qz[c][:, :PAIR]).astype(BF16) for c in chunks]
    z_c = [qz[c][:, PAIR:] for c in chunks]
    yield

    pairs = range(n_chunks // 2)
    comp = [_dot(m_c[2 * j + 1], jnp.concatenate([m_c[2 * j], n_c[2 * j].astype(BF16)], axis=1))
            for j in pairs]
    m_2 = [comp[j][:, :PAIR].astype(BF16) for j in pairs]
    n_2 = [comp[j][:, PAIR:] + n_c[2 * j + 1] for j in pairs]
    yield
    h = h_ref[q]
    outs = []
    for j in pairs:
        c0 = 2 * j
        c1 = c0 + 1
        h_b = h.astype(BF16)
        from_h = _dot(jnp.concatenate([m_2[j], m_c[c0], q_c[c0]], axis=0), h_b)
        h = from_h[:PAIR, :] + n_2[j]
        h_mid = (from_h[PAIR:2 * PAIR, :] + n_c[c0]).astype(BF16)
        o_0 = from_h[2 * PAIR:, :] + z_c[c0]
        o_1 = _dot(q_c[c1], h_mid) + z_c[c1]
        outs.append(o_0[:C, :] + o_0[C:, :])
        outs.append(o_1[:C, :] + o_1[C:, :])
        yield
    h_ref[q] = h
    o = jnp.concatenate(outs, axis=0)

    inv_n = 1.0 / RWKV_HEAD
    mu = head_sum(o, 2) * inv_n
    d = o - mu
    var = head_sum(d * d, 2) * inv_n
    y = d * lax.rsqrt(var + GN_EPS) * gn_g + gn_b
    y = y + head_sum(r * k2 * r_k, 1) * v
    o_ref[:, lanes] = (y * g).astype(o_ref.dtype)


def _rwkv_kernel(*refs, n_chunks):
    o_ref, h_ref = refs[-2:]

    @pl.when(pl.program_id(2) == 0)
    def _():
        h_ref[...] = jnp.zeros_like(h_ref)

    live = []
    for q in range(o_ref.shape[1] // PAIR):
        live.append(_rwkv_pair_stages(q, *refs, n_chunks))
        for _ in range(3):
            live = [gen for gen in live if next(gen, _DONE) is not _DONE]
    while live:
        live = [gen for gen in live if next(gen, _DONE) is not _DONE]


def _rwkv_mix(zs, lora, w_up, a_up, g_up, prm, batch, seq, d_rwkv, tt=512, pairs_per_step=2):
    M = zs.shape[0]
    wb = pairs_per_step * PAIR
    n_cols = d_rwkv // wb
    tt = _tile(seq, tt)
    nt = seq // tt
    zspec = lambda off: pl.BlockSpec((tt, wb), lambda b, p, t: (b * nt + t, off + p))
    wspec = lambda k: pl.BlockSpec((k, wb), lambda b, p, t: (0, p))
    return pl.pallas_call(
        functools.partial(_rwkv_kernel, n_chunks=tt // CHUNK),
        out_shape=jax.ShapeDtypeStruct((M, d_rwkv), BF16),
        grid=(batch, n_cols, nt),
        in_specs=[
            zspec(0), zspec(n_cols), zspec(2 * n_cols),
            pl.BlockSpec((tt, 4 * LANES), lambda b, p, t: (b * nt + t, 0)),
            wspec(LANES), wspec(LANES), wspec(2 * LANES), wspec(8),
        ],
        out_specs=pl.BlockSpec((tt, wb), lambda b, p, t: (b * nt + t, p)),
        scratch_shapes=[pltpu.VMEM((pairs_per_step, PAIR, PAIR), F32)],
        compiler_params=_params("parallel", "parallel", "arbitrary"),
    )(zs, zs, zs, lora, w_up, a_up, g_up, prm)


def _conv_kernel(u_ref, w_ref, cb_ref, g_ref, b_ref, o_ref, buf_ref, acc_ref, *, width):
    tt, ch = u_ref.shape
    t = pl.program_id(1)

    @pl.when(t == 0)
    def _():
        buf_ref[0:CONV_HALO, :] = jnp.zeros((CONV_HALO, ch), F32)

    buf_ref[CONV_HALO:, :] = u_ref[...]
    base = CONV_HALO - (width - 1)

    def slab(s, _):
        cols = pl.ds(pl.multiple_of(s * LANES, LANES), LANES)
        acc = jnp.broadcast_to(cb_ref[:, cols], (tt, LANES))
        for j in range(width):
            acc = acc + w_ref[pl.ds(j, 1), cols] * buf_ref[pl.ds(base + j, tt), cols]
        acc_ref[:, cols] = acc
        return 0

    lax.fori_loop(0, ch // LANES, slab, 0)
    buf_ref[0:CONV_HALO, :] = buf_ref[tt:tt + CONV_HALO, :]

    g = g_ref[...]
    b = b_ref[...]
    rows = 32

    def norm(i, _):
        sl = pl.ds(pl.multiple_of(i * rows, rows), rows)
        y = _layer_norm_rows(acc_ref[sl, :], g, b)
        o_ref[sl, :] = (y * _sigmoid(y)).astype(o_ref.dtype)
        return 0

    lax.fori_loop(0, tt // rows, norm, 0)


def _conv_module(u, conv_w, conv_b, ln_g, ln_b, batch, seq, width, tt=256):
    M, ch = u.shape
    tt = _tile(seq, tt)
    nt = seq // tt
    vec = lambda r: pl.BlockSpec((r, ch), lambda b, t: (0, 0))
    return pl.pallas_call(
        functools.partial(_conv_kernel, width=width),
        out_shape=jax.ShapeDtypeStruct((M, ch), BF16),
        grid=(batch, nt),
        in_specs=[pl.BlockSpec((tt, ch), lambda b, t: (b * nt + t, 0)),
                  vec(conv_w.shape[0]), vec(1), vec(1), vec(1)],
        out_specs=pl.BlockSpec((tt, ch), lambda b, t: (b * nt + t, 0)),
        scratch_shapes=[pltpu.VMEM((tt + CONV_HALO, ch), F32), pltpu.VMEM((tt, ch), F32)],
        compiler_params=_params("parallel", "arbitrary"),
    )(u, conv_w, conv_b, ln_g, ln_b)


def _qk_fold_kernel(wq_ref, k_ref, o_ref):
    o_ref[...] = _dot(wq_ref[...], k_ref[...], _NT).astype(o_ref.dtype)


def _qk_fold(wq, kx, batch, mem_len, heads):
    D = wq.shape[0]
    hd = D // heads
    return pl.pallas_call(
        _qk_fold_kernel,
        out_shape=jax.ShapeDtypeStruct((batch, D, heads * mem_len), BF16),
        grid=(heads, batch),
        in_specs=[pl.BlockSpec((D, hd), lambda h, b: (0, h)),
                  pl.BlockSpec((mem_len, hd), lambda h, b: (b, h))],
        out_specs=pl.BlockSpec((None, D, mem_len), lambda h, b: (b, 0, h)),
        compiler_params=_params("parallel", "arbitrary"),
    )(wq, kx)


def _vo_fold_kernel(v_ref, wo_ref, o_ref):
    o_ref[...] = _dot(v_ref[...], wo_ref[...]).astype(o_ref.dtype)


def _vo_fold(vx, wo, batch, mem_len, heads):
    D = wo.shape[0]
    hd = D // heads
    return pl.pallas_call(
        _vo_fold_kernel,
        out_shape=jax.ShapeDtypeStruct((batch, heads * mem_len, D), BF16),
        grid=(heads, batch),
        in_specs=[pl.BlockSpec((mem_len, hd), lambda h, b: (b, h)),
                  pl.BlockSpec((hd, D), lambda h, b: (h, 0))],
        out_specs=pl.BlockSpec((None, mem_len, D), lambda h, b: (b, h, 0)),
        compiler_params=_params("parallel", "arbitrary"),
    )(vx, wo)


def _attn_ln_kernel(x_ref, qk_ref, vo_ref, res_ref, g_ref, b_ref, o_ref, ob_ref, *, heads, scale, alpha):
    s = _dot(x_ref[...], qk_ref[...]) * scale
    mem_len = s.shape[1] // heads
    probs = []
    for h in range(heads):
        sh = s[:, h * mem_len:(h + 1) * mem_len]
        e = jnp.exp(sh - jnp.max(sh, axis=-1, keepdims=True))
        probs.append((e / jnp.sum(e, axis=-1, keepdims=True)).astype(BF16))
    p = jnp.concatenate(probs, axis=1)
    o_ref[...] = alpha * res_ref[...] + _dot(p, vo_ref[...])
    _ln_inplace(o_ref, ob_ref, g_ref, b_ref)


def _attn_ln(x, qk, vo, res, g, b, alpha, batch, seq, heads, tq=256):
    M, D = x.shape
    hm = qk.shape[2]
    tq = _tile(seq, tq)
    nt = seq // tq
    scale = float((D // heads) ** -0.5)
    single = pl.Buffered(1)
    tok = lambda: pl.BlockSpec((tq, D), lambda bi, t: (bi * nt + t, 0))
    vec = pl.BlockSpec((1, D), lambda bi, t: (0, 0))
    return pl.pallas_call(
        functools.partial(_attn_ln_kernel, heads=heads, scale=scale, alpha=alpha),
        out_shape=(jax.ShapeDtypeStruct((M, D), F32), jax.ShapeDtypeStruct((M, D), BF16)),
        grid=(batch, nt),
        in_specs=[tok(),
                  pl.BlockSpec((None, D, hm), lambda bi, t: (bi, 0, 0), pipeline_mode=single),
                  pl.BlockSpec((None, hm, D), lambda bi, t: (bi, 0, 0), pipeline_mode=single),
                  tok(), vec, vec],
        out_specs=(tok(), tok()),
        compiler_params=_params("parallel", "arbitrary"),
    )(x, qk, vo, res, g, b)


def _pad_cols(w, n):
    return jnp.pad(w, ((0, 0), (0, n - w.shape[1])))


def _pad_rows(w, n):
    return jnp.pad(w, ((0, n - w.shape[0]), (0, 0)))


def kernel(x, mem, w_in, rwkv_shift_mix, rwkv_w0, rwkv_w_up, rwkv_a0, rwkv_a_up, rwkv_g_up, rwkv_k_k, rwkv_k_a, rwkv_r_k, rwkv_gn_g, rwkv_gn_b, conv_w, conv_b, conv_ln_g, conv_ln_b, proj_rwkv, proj_conv, w_out, ln1_g, ln1_b, ln_mem_g, ln_mem_b, xattn_wq, xattn_wk, xattn_wv, xattn_wo, ln2_g, ln2_b, mlp_w1, mlp_w2, ln3_g, ln3_b):
    B, S, D = x.shape
    depth = w_in.shape[0]
    mem_len = mem.shape[1]
    d_rwkv = rwkv_w0.shape[1]
    d_conv = conv_b.shape[1]
    r_decay = rwkv_w_up.shape[1]
    r_iclr = rwkv_a_up.shape[1]
    r_gate = rwkv_g_up.shape[1]
    width = conv_w.shape[1]
    assert r_decay <= LANES and r_iclr <= LANES and r_gate == 2 * LANES
    assert d_rwkv % PAIR == 0 and width - 1 <= CONV_HALO
    alpha = float((2 * depth) ** 0.25)
    M = B * S
    row = lambda a: a.reshape(1, -1)

    mem_n = _ln_rows(mem.reshape(B * mem_len, D), row(ln_mem_g), row(ln_mem_b), BF16)
    h = x.reshape(M, D)
    hb = h.astype(BF16)
    for l in range(depth):
        w = w_in[l]
        c_lora = 3 * d_rwkv
        c_conv = c_lora + r_decay + r_iclr + r_gate
        c_gate = c_conv + 2 * d_conv
        w_rwkv = jnp.concatenate([
            w[:, :c_lora],
            _pad_cols(w[:, c_lora:c_lora + r_decay], LANES),
            _pad_cols(w[:, c_lora + r_decay:c_lora + r_decay + r_iclr], LANES),
            w[:, c_lora + r_decay + r_iclr:c_conv]], axis=1).astype(BF16)
        mix = rwkv_shift_mix[l]
        mu = jnp.concatenate([
            mix[:c_lora],
            jnp.pad(mix[c_lora:c_lora + r_decay], (0, LANES - r_decay)),
            jnp.pad(mix[c_lora + r_decay:c_lora + r_decay + r_iclr], (0, LANES - r_iclr)),
            mix[c_lora + r_decay + r_iclr:]]).reshape(1, -1)
        zs = _mm_shift(hb, w_rwkv, mu, S)
        lora = _lora_act(zs, c_lora // (4 * LANES))
        prm = jnp.stack([rwkv_w0[l], rwkv_a0[l], rwkv_k_k[l], rwkv_k_a[l], rwkv_r_k[l].reshape(-1),
                         rwkv_gn_g[l], rwkv_gn_b[l], jnp.zeros((d_rwkv,), F32)])
        o_r = _rwkv_mix(zs, lora, _pad_rows(rwkv_w_up[l], LANES), _pad_rows(rwkv_a_up[l], LANES),
                        rwkv_g_up[l], prm, B, S, d_rwkv)

        u = _mm_glu(hb, w[:, c_conv:c_conv + d_conv].astype(BF16),
                    w[:, c_conv + d_conv:c_gate].astype(BF16))
        cw = _pad_rows(conv_w[l].reshape(width, d_conv), CONV_HALO)
        o_c = _conv_module(u, cw, row(conv_b[l]), row(conv_ln_g[l]), row(conv_ln_b[l]), B, S, width)

        merged = _mm_merge(hb, o_r, o_c, w[:, c_gate:c_gate + D].astype(BF16),
                           w[:, c_gate + D:].astype(BF16),
                           proj_rwkv[l].astype(BF16), proj_conv[l].astype(BF16))
        h, hb = _mm_res_ln(merged, w_out[l], h, row(ln1_g[l]), row(ln1_b[l]), alpha)

        kx = _mm_plain(mem_n, xattn_wk[l].astype(BF16), BF16)
        vx = _mm_plain(mem_n, xattn_wv[l].astype(BF16), BF16)
        qk = _qk_fold(xattn_wq[l].astype(BF16), kx, B, mem_len, XATTN_HEADS)
        vo = _vo_fold(vx, xattn_wo[l].astype(BF16), B, mem_len, XATTN_HEADS)
        h, hb = _attn_ln(hb, qk, vo, h, row(ln2_g[l]), row(ln2_b[l]), alpha, B, S, XATTN_HEADS)

        h = _mlp_ln(hb, mlp_w1[l], mlp_w2[l], h, row(ln3_g[l]), row(ln3_b[l]), alpha)
        hb = h.astype(BF16)
    return h.reshape(B, S, D)
```

```python
import functools

import jax
import jax.numpy as jnp
from jax import lax
from jax.experimental import pallas as pl
from jax.experimental.pallas import tpu as pltpu

F32 = jnp.float32
BF16 = jnp.bfloat16

LANES = 128
RWKV_HEAD = 64
PAIR = 2 * RWKV_HEAD
CHUNK = 64
CONV_HALO = 32
XATTN_HEADS = 4
LN_EPS = 1e-5
GN_EPS = 64e-5
VMEM_LIMIT = 56 * 1024 * 1024

_NN = (((1,), (0,)), ((), ()))
_NT = (((1,), (1,)), ((), ()))
_TN = (((0,), (0,)), ((), ()))
_DONE = object()


def _params(*sem):
    return pltpu.CompilerParams(dimension_semantics=sem, vmem_limit_bytes=VMEM_LIMIT)


def _dot(a, b, dims=_NN):
    return lax.dot_general(a, b, dims, preferred_element_type=F32)


def _sigmoid(x):
    return 1.0 / (1.0 + jnp.exp(-x))


def _layer_norm_rows(v, g, b):
    mu = jnp.mean(v, axis=-1, keepdims=True)
    d = v - mu
    var = jnp.mean(d * d, axis=-1, keepdims=True)
    return d * lax.rsqrt(var + LN_EPS) * g + b


def _tile(n, pref):
    t = min(n, pref)
    assert n % t == 0, (n, pref)
    return t


def _mm_shift_kernel(x_ref, w_ref, mu_ref, o_ref, carry_ref, *, tiles_per_seq):
    m = pl.program_id(1)
    z = _dot(x_ref[...], w_ref[...])
    tm = z.shape[0]

    @pl.when(m % tiles_per_seq == 0)
    def _():
        carry_ref[...] = jnp.zeros_like(carry_ref)

    prev_row = carry_ref[0:1, :]
    row = lax.broadcasted_iota(jnp.int32, z.shape, 0)
    prev = jnp.where(row == 0, prev_row, pltpu.roll(z, 1, axis=0))
    o_ref[...] = z + (prev - z) * mu_ref[...]
    carry_ref[0:1, :] = z[tm - 1:tm, :]


def _mm_shift(x, w, mu, seq, tm=1024, tn=512):
    M, K = x.shape
    N = w.shape[1]
    tm = _tile(seq, tm)
    tn = _tile(N, tn)
    return pl.pallas_call(
        functools.partial(_mm_shift_kernel, tiles_per_seq=seq // tm),
        out_shape=jax.ShapeDtypeStruct((M, N), F32),
        grid=(N // tn, M // tm),
        in_specs=[
            pl.BlockSpec((tm, K), lambda n, m: (m, 0)),
            pl.BlockSpec((K, tn), lambda n, m: (0, n)),
            pl.BlockSpec((1, tn), lambda n, m: (0, n)),
        ],
        out_specs=pl.BlockSpec((tm, tn), lambda n, m: (m, n)),
        scratch_shapes=[pltpu.VMEM((8, tn), F32)],
        compiler_params=_params("parallel", "arbitrary"),
    )(x, w, mu)


def _mm_glu_kernel(x_ref, wa_ref, wb_ref, o_ref):
    x = x_ref[...]
    o_ref[...] = _dot(x, wa_ref[...]) * _sigmoid(_dot(x, wb_ref[...]))


def _mm_glu(x, wa, wb, tm=1024, tn=256):
    M, K = x.shape
    N = wa.shape[1]
    tm = _tile(M, tm)
    tn = _tile(N, tn)
    return pl.pallas_call(
        _mm_glu_kernel,
        out_shape=jax.ShapeDtypeStruct((M, N), F32),
        grid=(N // tn, M // tm),
        in_specs=[
            pl.BlockSpec((tm, K), lambda n, m: (m, 0)),
            pl.BlockSpec((K, tn), lambda n, m: (0, n)),
            pl.BlockSpec((K, tn), lambda n, m: (0, n)),
        ],
        out_specs=pl.BlockSpec((tm, tn), lambda n, m: (m, n)),
        compiler_params=_params("parallel", "arbitrary"),
    )(x, wa, wb)


def _mm_plain_kernel(x_ref, w_ref, o_ref):
    o_ref[...] = _dot(x_ref[...], w_ref[...]).astype(o_ref.dtype)


def _mm_plain(x, w, out_dtype, tm=1024, tn=512):
    M, K = x.shape
    N = w.shape[1]
    tm = _tile(M, tm)
    tn = _tile(N, tn)
    return pl.pallas_call(
        _mm_plain_kernel,
        out_shape=jax.ShapeDtypeStruct((M, N), out_dtype),
        grid=(N // tn, M // tm),
        in_specs=[
            pl.BlockSpec((tm, K), lambda n, m: (m, 0)),
            pl.BlockSpec((K, tn), lambda n, m: (0, n)),
        ],
        out_specs=pl.BlockSpec((tm, tn), lambda n, m: (m, n)),
        compiler_params=_params("parallel", "arbitrary"),
    )(x, w)


def _mm_merge_kernel(x_ref, orw_ref, ocv_ref, wgr_ref, wgc_ref, pr_ref, pc_ref, o_ref):
    x = x_ref[...]
    gate_r = _sigmoid(_dot(x, wgr_ref[...]))
    gate_c = _sigmoid(_dot(x, wgc_ref[...]))
    merged = gate_r * _dot(orw_ref[...], pr_ref[...]) + gate_c * _dot(ocv_ref[...], pc_ref[...])
    o_ref[...] = merged.astype(o_ref.dtype)


def _mm_merge(x, o_r, o_c, wgr, wgc, p_r, p_c, tm=512, tn=512):
    M, K = x.shape
    Kb = o_r.shape[1]
    N = wgr.shape[1]
    tm = _tile(M, tm)
    tn = _tile(N, tn)
    row = lambda k: pl.BlockSpec((tm, k), lambda n, m: (m, 0))
    col = lambda k: pl.BlockSpec((k, tn), lambda n, m: (0, n))
    return pl.pallas_call(
        _mm_merge_kernel,
        out_shape=jax.ShapeDtypeStruct((M, N), BF16),
        grid=(N // tn, M // tm),
        in_specs=[row(K), row(Kb), row(Kb), col(K), col(K), col(Kb), col(Kb)],
        out_specs=pl.BlockSpec((tm, tn), lambda n, m: (m, n)),
        compiler_params=_params("parallel", "arbitrary"),
    )(x, o_r, o_c, wgr, wgc, p_r, p_c)


def _ln_inplace(o_ref, ob_ref, g_ref, b_ref, rows=128):
    g = g_ref[...]
    b = b_ref[...]

    def body(i, _):
        sl = pl.ds(pl.multiple_of(i * rows, rows), rows)
        y = _layer_norm_rows(o_ref[sl, :], g, b)
        o_ref[sl, :] = y
        if ob_ref is not None:
            ob_ref[sl, :] = y.astype(ob_ref.dtype)
        return 0

    lax.fori_loop(0, o_ref.shape[0] // rows, body, 0)


def _mm_res_ln_kernel(y_ref, w_ref, res_ref, g_ref, b_ref, o_ref, ob_ref, *, alpha, tn):
    n = pl.program_id(1)
    col = pl.ds(pl.multiple_of(n * tn, tn), tn)
    o_ref[:, col] = alpha * res_ref[...] + _dot(y_ref[...], w_ref[...])

    @pl.when(n == pl.num_programs(1) - 1)
    def _():
        _ln_inplace(o_ref, ob_ref, g_ref, b_ref)


def _mm_res_ln(y, w, res, g, b, alpha, tm=512, tn=512):
    M, K = y.shape
    N = w.shape[1]
    tm = _tile(M, tm)
    tn = _tile(N, tn)
    return pl.pallas_call(
        functools.partial(_mm_res_ln_kernel, alpha=alpha, tn=tn),
        out_shape=(jax.ShapeDtypeStruct((M, N), F32), jax.ShapeDtypeStruct((M, N), BF16)),
        grid=(M // tm, N // tn),
        in_specs=[
            pl.BlockSpec((tm, K), lambda m, n: (m, 0)),
            pl.BlockSpec((K, tn), lambda m, n: (0, n)),
            pl.BlockSpec((tm, tn), lambda m, n: (m, n)),
            pl.BlockSpec((1, N), lambda m, n: (0, 0)),
            pl.BlockSpec((1, N), lambda m, n: (0, 0)),
        ],
        out_specs=(pl.BlockSpec((tm, N), lambda m, n: (m, 0)),
                   pl.BlockSpec((tm, N), lambda m, n: (m, 0))),
        compiler_params=_params("parallel", "arbitrary"),
    )(y, w.astype(BF16), res, g, b)


def _mlp_ln_kernel(x_ref, w1_ref, w2_ref, res_ref, g_ref, b_ref, o_ref, *, alpha, tn):
    k = pl.program_id(1)
    h = jnp.maximum(_dot(x_ref[...], w1_ref[...]), 0.0)
    h = (h * h).astype(BF16)
    n_slabs = o_ref.shape[1] // tn

    @pl.when(k == 0)
    def _():
        for j in range(n_slabs):
            sl = slice(j * tn, (j + 1) * tn)
            o_ref[:, sl] = _dot(h, w2_ref[:, sl])

    @pl.when(k > 0)
    def _():
        for j in range(n_slabs):
            sl = slice(j * tn, (j + 1) * tn)
            o_ref[:, sl] += _dot(h, w2_ref[:, sl])

    @pl.when(k < n_slabs)
    def _():
        col = pl.ds(pl.multiple_of(k * tn, tn), tn)
        o_ref[:, col] += alpha * res_ref[...]

    @pl.when(k == pl.num_programs(1) - 1)
    def _():
        _ln_inplace(o_ref, None, g_ref, b_ref)


def _mlp_ln(x, w1, w2, res, g, b, alpha, tm=1024, tk=512, tn=512):
    M, D = x.shape
    FF = w1.shape[1]
    tm = _tile(M, tm)
    tk = _tile(FF, tk)
    tn = _tile(D, tn)
    n_slabs = D // tn
    assert FF // tk >= n_slabs
    single = pl.Buffered(1)
    return pl.pallas_call(
        functools.partial(_mlp_ln_kernel, alpha=alpha, tn=tn),
        out_shape=jax.ShapeDtypeStruct((M, D), F32),
        grid=(M // tm, FF // tk),
        in_specs=[
            pl.BlockSpec((tm, D), lambda m, k: (m, 0), pipeline_mode=single),
            pl.BlockSpec((D, tk), lambda m, k: (0, k)),
            pl.BlockSpec((tk, D), lambda m, k: (k, 0)),
            pl.BlockSpec((tm, tn), lambda m, k: (m, jnp.minimum(k, n_slabs - 1))),
            pl.BlockSpec((1, D), lambda m, k: (0, 0)),
            pl.BlockSpec((1, D), lambda m, k: (0, 0)),
        ],
        out_specs=pl.BlockSpec((tm, D), lambda m, k: (m, 0), pipeline_mode=single),
        compiler_params=_params("parallel", "arbitrary"),
    )(x, w1.astype(BF16), w2.astype(BF16), res, g, b)


def _ln_rows_kernel(x_ref, g_ref, b_ref, o_ref):
    o_ref[...] = _layer_norm_rows(x_ref[...], g_ref[...], b_ref[...]).astype(o_ref.dtype)


def _ln_rows(x, g, b, out_dtype, tm=128):
    M, D = x.shape
    tm = _tile(M, tm)
    return pl.pallas_call(
        _ln_rows_kernel,
        out_shape=jax.ShapeDtypeStruct((M, D), out_dtype),
        grid=(M // tm,),
        in_specs=[pl.BlockSpec((tm, D), lambda m: (m, 0)),
                  pl.BlockSpec((1, D), lambda m: (0, 0)),
                  pl.BlockSpec((1, D), lambda m: (0, 0))],
        out_specs=pl.BlockSpec((tm, D), lambda m: (m, 0)),
        compiler_params=_params("parallel"),
    )(x, g, b)


def _lora_act_kernel(z_ref, o_ref):
    o_ref[:, 0:LANES] = jnp.tanh(z_ref[:, 0:LANES])
    o_ref[:, LANES:2 * LANES] = z_ref[:, LANES:2 * LANES]
    o_ref[:, 2 * LANES:] = _sigmoid(z_ref[:, 2 * LANES:])


def _lora_act(zs, col_block, tm=1024):
    M = zs.shape[0]
    W = 4 * LANES
    tm = _tile(M, tm)
    return pl.pallas_call(
        _lora_act_kernel,
        out_shape=jax.ShapeDtypeStruct((M, W), F32),
        grid=(M // tm,),
        in_specs=[pl.BlockSpec((tm, W), lambda m: (m, col_block))],
        out_specs=pl.BlockSpec((tm, W), lambda m: (m, 0)),
        compiler_params=_params("parallel"),
    )(zs)


def _rwkv_pair_stages(q, zr_ref, zk_ref, zv_ref, l_ref, wup_ref, aup_ref, gup_ref, prm_ref, o_ref, h_ref,
                      n_chunks):
    C = CHUNK
    lane = lax.broadcasted_iota(jnp.int32, (1, PAIR), 1)
    m0 = (lane < RWKV_HEAD).astype(F32)
    m1 = 1.0 - m0
    ri = lax.broadcasted_iota(jnp.int32, (PAIR, PAIR), 0)
    ci = lax.broadcasted_iota(jnp.int32, (PAIR, PAIR), 1)
    strict = ci < ri
    lower = ci <= ri
    head_bits = RWKV_HEAD.bit_length() - 1
    same_head = jnp.right_shift(ri, head_bits) == jnp.right_shift(ci, head_bits)
    eye = (ri == ci).astype(F32)
    head_ones = same_head.astype(BF16)
    rc = lax.broadcasted_iota(jnp.int32, (C, C), 0)
    cc = lax.broadcasted_iota(jnp.int32, (C, C), 1)
    tril_c = (cc <= rc).astype(BF16)
    assert n_chunks % 2 == 0

    lanes = slice(q * PAIR, (q + 1) * PAIR)
    prm = prm_ref[:, lanes]
    w0, a0, k_k, k_a, r_k, gn_g, gn_b = (prm[i:i + 1, :] for i in range(7))
    w_up = wup_ref[:, lanes].astype(BF16)
    a_up = aup_ref[:, lanes].astype(BF16)
    g_up = gup_ref[:, lanes].astype(BF16)

    def stack(x):
        return jnp.concatenate([x * m0, x * m1], axis=0)

    def head_sum(x, passes):
        acc = None
        for _ in range(passes):
            piece = x.astype(BF16)
            part = _dot(piece, head_ones)
            acc = part if acc is None else acc + part
            x = x - piece.astype(F32)
        return acc

    def bdot(a, b, dims=_NN):
        return _dot(a.astype(BF16), b.astype(BF16), dims)

    r = zr_ref[:, lanes]
    k = zk_ref[:, lanes]
    v = zv_ref[:, lanes]
    lora = l_ref[...].astype(BF16)
    w_raw = w0 + _dot(lora[:, 0:LANES], w_up)
    a_raw = a0 + _dot(lora[:, LANES:2 * LANES], a_up)
    g = _dot(lora[:, 2 * LANES:], g_up)
    w = -(jnp.maximum(-w_raw, 0.0) + jnp.log(1.0 + jnp.exp(-jnp.abs(w_raw)))) - 0.5
    wl = -jnp.exp(w)
    a_sig = _sigmoid(a_raw)
    kk = k * k_k
    kk = kk / jnp.maximum(jnp.sqrt(head_sum(kk * kk, 1)), 1e-12)
    k2 = k * (1.0 + (a_sig - 1.0) * k_a)
    b_s = kk * a_sig
    yield

    chunks = range(n_chunks)
    rows = [slice(c * C, (c + 1) * C) for c in chunks]
    wl_hi = wl.astype(BF16)
    wl_r = wl - wl_hi.astype(F32)
    wl_mid = wl_r.astype(BF16)
    wl_lo = (wl_r - wl_mid.astype(F32)).astype(BF16)
    wl_parts = jnp.concatenate([wl_hi, wl_mid, wl_lo], axis=1)
    cum_parts = [_dot(tril_c, wl_parts[s]) for s in rows]
    cum_c = [m[:, :PAIR] + (m[:, PAIR:2 * PAIR] + m[:, 2 * PAIR:]) for m in cum_parts]
    cum = jnp.concatenate(cum_c, axis=0)
    cum_end = jnp.concatenate([jnp.broadcast_to(m[C - 1:C, :], (C, PAIR)) for m in cum_c], axis=0)
    e_neg = jnp.exp(-cum)
    e_tail = jnp.exp(cum_end - cum)
    r_t = r * jnp.exp(cum)
    a_t = -kk * jnp.exp(cum - wl)
    b_t = b_s * e_neg
    k_t = k2 * e_neg
    bw = b_s * e_tail
    kw = k2 * e_tail
    w_end = jnp.exp(cum_end)
    yield

    a_st = [stack(a_t[s]).astype(BF16) for s in rows]
    r_st = [stack(r_t[s]) for s in rows]
    v_st = [stack(v[s]).astype(BF16) for s in rows]
    gram = [bdot(jnp.concatenate([a_st[c], r_st[c].astype(BF16)], axis=0),
                 jnp.concatenate([stack(b_t[rows[c]]), stack(k_t[rows[c]])], axis=0), _NT)
            for c in chunks]
    yield
    a_ab =[jnp.where(strict, m[:PAIR, :PAIR], 0.0) for m in gram]
    a_ak = [jnp.where(strict, m[:PAIR, PAIR:], 0.0).astype(BF16) for m in gram]
    g_rb = [jnp.where(lower, m[PAIR:, :PAIR], 0.0).astype(BF16) for m in gram]
    g_rk = [jnp.where(lower, m[PAIR:, PAIR:], 0.0).astype(BF16) for m in gram]

    pw = [m.astype(BF16) for m in a_ab]
    pw = [_dot(m, m).astype(BF16) for m in pw]
    t_inv = [eye + m for m in a_ab]
    yield
    n = 2
    while 2 * n < C:
        both = [_dot(pw[c], jnp.concatenate([pw[c], t_inv[c].astype(BF16)], axis=1)) for c in chunks]
        pw = [m[:, :PAIR].astype(BF16) for m in both]
        t_inv = [t_inv[c] + both[c][:, PAIR:] for c in chunks]
        n *= 2
        yield
    t_b = [(t_inv[c] + _dot(pw[c], t_inv[c].astype(BF16))).astype(BF16) for c in chunks]

    a_v = [_dot(a_ak[c], v_st[c]).astype(BF16) for c in chunks]
    yield
    t_av =[_dot(t_b[c], jnp.concatenate([a_st[c], a_v[c]], axis=1)).astype(BF16)
            for c in chunks]
    zero = jnp.zeros((PAIR, PAIR), BF16)
    rhs2 = [jnp.concatenate([t_av[c], jnp.concatenate([zero, v_st[c]], axis=1)], axis=0)
            for c in chunks]
    yield
    bw_st = [stack(bw[s]).astype(BF16) for s in rows]
    kw_st = [stack(kw[s]).astype(BF16) for s in rows]
    mn = [_dot(jnp.concatenate([bw_st[c], kw_st[c]], axis=0), rhs2[c], _TN) for c in chunks]
    qz = [_dot(jnp.concatenate([g_rb[c], g_rk[c]], axis=1), rhs2[c]) for c in chunks]
    m_c = [(eye * w_end[rows[c]][0:1, :] + mn[c][:, :PAIR]).astype(BF16) for c in chunks]
    n_c = [mn[c][:, PAIR:] for c in chunks]
    q_c = [(r_st[c] + qz[c][:, :PAIR]).astype(BF16) for c in chunks]
    z_c = [qz[c][:, PAIR:] for c in chunks]
    yield

    pairs = range(n_chunks // 2)
    comp = [_dot(m_c[2 * j + 1], jnp.concatenate([m_c[2 * j], n_c[2 * j].astype(BF16)], axis=1))
            for j in pairs]
    m_2 = [comp[j][:, :PAIR].astype(BF16) for j in pairs]
    n_2 = [comp[j][:, PAIR:] + n_c[2 * j + 1] for j in pairs]
    yield
    h = h_ref[q]
    outs = []
    for j in pairs:
        c0 = 2 * j
        c1 = c0 + 1
        h_b = h.astype(BF16)
        from_h = _dot(jnp.concatenate([m_2[j], m_c[c0], q_c[c0]], axis=0), h_b)
        h = from_h[:PAIR, :] + n_2[j]
        h_mid = (from_h[PAIR:2 * PAIR, :] + n_c[c0]).astype(BF16)
        o_0 = from_h[2 * PAIR:, :] + z_c[c0]
        o_1 = _dot(q_c[c1], h_mid) + z_c[c1]
        outs.append(o_0[:C, :] + o_0[C:, :])
        outs.append(o_1[:C, :] + o_1[C:, :])
        yield
    h_ref[q] = h
    o = jnp.concatenate(outs, axis=0)

    inv_n = 1.0 / RWKV_HEAD
    mu = head_sum(o, 2) * inv_n
    d = o - mu
    var = head_sum(d * d, 2) * inv_n
    y = d * lax.rsqrt(var + GN_EPS) * gn_g + gn_b
    y = y + head_sum(r * k2 * r_k, 1) * v
    o_ref[:, lanes] = (y * g).astype(o_ref.dtype)


def _rwkv_kernel(*refs, n_chunks):
    o_ref, h_ref = refs[-2:]

    @pl.when(pl.program_id(2) == 0)
    def _():
        h_ref[...] = jnp.zeros_like(h_ref)

    live = []
    for q in range(o_ref.shape[1] // PAIR):
        live.append(_rwkv_pair_stages(q, *refs, n_chunks))
        for _ in range(3):
            live = [gen for gen in live if next(gen, _DONE) is not _DONE]
    while live:
        live = [gen for gen in live if next(gen, _DONE) is not _DONE]


def _rwkv_mix(zs, lora, w_up, a_up, g_up, prm, batch, seq, d_rwkv, tt=512, pairs_per_step=2):
    M = zs.shape[0]
    wb = pairs_per_step * PAIR
    n_cols = d_rwkv // wb
    tt = _tile(seq, tt)
    nt = seq // tt
    zspec = lambda off: pl.BlockSpec((tt, wb), lambda b, p, t: (b * nt + t, off + p))
    wspec = lambda k: pl.BlockSpec((k, wb), lambda b, p, t: (0, p))
    return pl.pallas_call(
        functools.partial(_rwkv_kernel, n_chunks=tt // CHUNK),
        out_shape=jax.ShapeDtypeStruct((M, d_rwkv), BF16),
        grid=(batch, n_cols, nt),
        in_specs=[
            zspec(0), zspec(n_cols), zspec(2 * n_cols),
            pl.BlockSpec((tt, 4 * LANES), lambda b, p, t: (b * nt + t, 0)),
            wspec(LANES), wspec(LANES), wspec(2 * LANES), wspec(8),
        ],
        out_specs=pl.BlockSpec((tt, wb), lambda b, p, t: (b * nt + t, p)),
        scratch_shapes=[pltpu.VMEM((pairs_per_step, PAIR, PAIR), F32)],
        compiler_params=_params("parallel", "parallel", "arbitrary"),
    )(zs, zs, zs, lora, w_up, a_up, g_up, prm)


def _conv_kernel(u_ref, w_ref, cb_ref, g_ref, b_ref, o_ref, buf_ref, acc_ref, win_ref, *, width):
    tt, ch = u_ref.shape
    t = pl.program_id(1)

    @pl.when(t == 0)
    def _():
        buf_ref[0:CONV_HALO, :] = jnp.zeros((CONV_HALO, ch), F32)

    buf_ref[CONV_HALO:, :] = u_ref[...]
    base = CONV_HALO - (width - 1)

    sub = 8

    def slab(s, _):
        cols = pl.ds(pl.multiple_of(s * LANES, LANES), LANES)
        acc = jnp.broadcast_to(cb_ref[:, cols], (tt, LANES))
        for res in range(sub):
            taps = [j for j in range(width) if (base + j) % sub == res]
            if not taps:
                continue
            span = (base + taps[-1]) // sub * sub
            win_ref[0:tt + span, :] = buf_ref[pl.ds(res, tt + span), cols]
            for j in taps:
                off = base + j - res
                acc = acc + w_ref[pl.ds(j, 1), cols] * win_ref[off:off + tt, :]
        acc_ref[:, cols] = acc
        return 0

    lax.fori_loop(0, ch // LANES, slab, 0)
    buf_ref[0:CONV_HALO, :] = buf_ref[tt:tt + CONV_HALO, :]

    g = g_ref[...]
    b = b_ref[...]
    rows = min(tt, 128)

    def norm(i, _):
        sl = pl.ds(pl.multiple_of(i * rows, rows), rows)
        y = _layer_norm_rows(acc_ref[sl, :], g, b)
        o_ref[sl, :] = (y * _sigmoid(y)).astype(o_ref.dtype)
        return 0

    lax.fori_loop(0, tt // rows, norm, 0)


def _conv_module(u, conv_w, conv_b, ln_g, ln_b, batch, seq, width, tt=256):
    M, ch = u.shape
    tt = _tile(seq, tt)
    nt = seq // tt
    vec = lambda r: pl.BlockSpec((r, ch), lambda b, t: (0, 0))
    return pl.pallas_call(
        functools.partial(_conv_kernel, width=width),
        out_shape=jax.ShapeDtypeStruct((M, ch), BF16),
        grid=(batch, nt),
        in_specs=[pl.BlockSpec((tt, ch), lambda b, t: (b * nt + t, 0)),
                  vec(conv_w.shape[0]), vec(1), vec(1), vec(1)],
        out_specs=pl.BlockSpec((tt, ch), lambda b, t: (b * nt + t, 0)),
        scratch_shapes=[pltpu.VMEM((tt + CONV_HALO, ch), F32), pltpu.VMEM((tt, ch), F32),
                        pltpu.VMEM((tt + CONV_HALO, LANES), F32)],
        compiler_params=_params("parallel", "arbitrary"),
    )(u, conv_w, conv_b, ln_g, ln_b)


def _qk_fold_kernel(wq_ref, k_ref, o_ref):
    o_ref[...] = _dot(wq_ref[...], k_ref[...], _NT).astype(o_ref.dtype)


def _qk_fold(wq, kx, batch, mem_len, heads):
    D = wq.shape[0]
    hd = D // heads
    return pl.pallas_call(
        _qk_fold_kernel,
        out_shape=jax.ShapeDtypeStruct((batch, D, heads * mem_len), BF16),
        grid=(heads, batch),
        in_specs=[pl.BlockSpec((D, hd), lambda h, b: (0, h)),
                  pl.BlockSpec((mem_len, hd), lambda h, b: (b, h))],
        out_specs=pl.BlockSpec((None, D, mem_len), lambda h, b: (b, 0, h)),
        compiler_params=_params("parallel", "arbitrary"),
    )(wq, kx)


def _vo_fold_kernel(v_ref, wo_ref, o_ref):
    o_ref[...] = _dot(v_ref[...], wo_ref[...]).astype(o_ref.dtype)


def _vo_fold(vx, wo, batch, mem_len, heads):
    D = wo.shape[0]
    hd = D // heads
    return pl.pallas_call(
        _vo_fold_kernel,
        out_shape=jax.ShapeDtypeStruct((batch, heads * mem_len, D), BF16),
        grid=(heads, batch),
        in_specs=[pl.BlockSpec((mem_len, hd), lambda h, b: (b, h)),
                  pl.BlockSpec((hd, D), lambda h, b: (h, 0))],
        out_specs=pl.BlockSpec((None, mem_len, D), lambda h, b: (b, h, 0)),
        compiler_params=_params("parallel", "arbitrary"),
    )(vx, wo)


def _attn_ln_kernel(x_ref, qk_ref, vo_ref, res_ref, g_ref, b_ref, o_ref, ob_ref, *, heads, scale, alpha):
    s = _dot(x_ref[...], qk_ref[...]) * scale
    mem_len = s.shape[1] // heads
    probs = []
    for h in range(heads):
        sh = s[:, h * mem_len:(h + 1) * mem_len]
        e = jnp.exp(sh - jnp.max(sh, axis=-1, keepdims=True))
        probs.append((e / jnp.sum(e, axis=-1, keepdims=True)).astype(BF16))
    p = jnp.concatenate(probs, axis=1)
    o_ref[...] = alpha * res_ref[...] + _dot(p, vo_ref[...])
    _ln_inplace(o_ref, ob_ref, g_ref, b_ref)


def _attn_ln(x, qk, vo, res, g, b, alpha, batch, seq, heads, tq=256):
    M, D = x.shape
    hm = qk.shape[2]
    tq = _tile(seq, tq)
    nt = seq // tq
    scale = float((D // heads) ** -0.5)
    single = pl.Buffered(1)
    tok = lambda: pl.BlockSpec((tq, D), lambda bi, t: (bi * nt + t, 0))
    vec = pl.BlockSpec((1, D), lambda bi, t: (0, 0))
    return pl.pallas_call(
        functools.partial(_attn_ln_kernel, heads=heads, scale=scale, alpha=alpha),
        out_shape=(jax.ShapeDtypeStruct((M, D), F32), jax.ShapeDtypeStruct((M, D), BF16)),
        grid=(batch, nt),
        in_specs=[tok(),
                  pl.BlockSpec((None, D, hm), lambda bi, t: (bi, 0, 0), pipeline_mode=single),
                  pl.BlockSpec((None, hm, D), lambda bi, t: (bi, 0, 0), pipeline_mode=single),
                  tok(), vec, vec],
        out_specs=(tok(), tok()),
        compiler_params=_params("parallel", "arbitrary"),
    )(x, qk, vo, res, g, b)


def _pad_cols(w, n):
    return jnp.pad(w, ((0, 0), (0, n - w.shape[1])))


def _pad_rows(w, n):
    return jnp.pad(w, ((0, n - w.shape[0]), (0, 0)))


def kernel(x, mem, w_in, rwkv_shift_mix, rwkv_w0, rwkv_w_up, rwkv_a0, rwkv_a_up, rwkv_g_up, rwkv_k_k, rwkv_k_a, rwkv_r_k, rwkv_gn_g, rwkv_gn_b, conv_w, conv_b, conv_ln_g, conv_ln_b, proj_rwkv, proj_conv, w_out, ln1_g, ln1_b, ln_mem_g, ln_mem_b, xattn_wq, xattn_wk, xattn_wv, xattn_wo, ln2_g, ln2_b, mlp_w1, mlp_w2, ln3_g, ln3_b):
    B, S, D = x.shape
    depth = w_in.shape[0]
    mem_len = mem.shape[1]
    d_rwkv = rwkv_w0.shape[1]
    d_conv = conv_b.shape[1]
    r_decay = rwkv_w_up.shape[1]
    r_iclr = rwkv_a_up.shape[1]
    r_gate = rwkv_g_up.shape[1]
    width = conv_w.shape[1]
    assert r_decay <= LANES and r_iclr <= LANES and r_gate == 2 * LANES
    assert d_rwkv % PAIR == 0 and width - 1 <= CONV_HALO
    alpha = float((2 * depth) ** 0.25)
    M = B * S
    row = lambda a: a.reshape(1, -1)

    mem_n = _ln_rows(mem.reshape(B * mem_len, D), row(ln_mem_g), row(ln_mem_b), BF16)
    h = x.reshape(M, D)
    hb = h.astype(BF16)
    for l in range(depth):
        w = w_in[l]
        c_lora = 3 * d_rwkv
        c_conv = c_lora + r_decay + r_iclr + r_gate
        c_gate = c_conv + 2 * d_conv
        mix = rwkv_shift_mix[l]
        zs = _mm_shift(hb, w[:, :c_lora].astype(BF16), row(mix[:c_lora]), S)
        w_lora = jnp.concatenate([
            _pad_cols(w[:, c_lora:c_lora + r_decay], LANES),
            _pad_cols(w[:, c_lora + r_decay:c_lora + r_decay + r_iclr], LANES),
            w[:, c_lora + r_decay + r_iclr:c_conv]], axis=1).astype(BF16)
        mu_lora = jnp.concatenate([
            jnp.pad(mix[c_lora:c_lora + r_decay], (0, LANES - r_decay)),
            jnp.pad(mix[c_lora + r_decay:c_lora + r_decay + r_iclr], (0, LANES - r_iclr)),
            mix[c_lora + r_decay + r_iclr:]]).reshape(1, -1)
        lora = _lora_act(_mm_shift(hb, w_lora, mu_lora, S), 0)
        prm = jnp.stack([rwkv_w0[l], rwkv_a0[l], rwkv_k_k[l], rwkv_k_a[l], rwkv_r_k[l].reshape(-1),
                         rwkv_gn_g[l], rwkv_gn_b[l], jnp.zeros((d_rwkv,), F32)])
        o_r = _rwkv_mix(zs, lora, _pad_rows(rwkv_w_up[l], LANES), _pad_rows(rwkv_a_up[l], LANES),
                        rwkv_g_up[l], prm, B, S, d_rwkv)

        u = _mm_glu(hb, w[:, c_conv:c_conv + d_conv].astype(BF16),
                    w[:, c_conv + d_conv:c_gate].astype(BF16))
        cw = _pad_rows(conv_w[l].reshape(width, d_conv), CONV_HALO)
        o_c = _conv_module(u, cw, row(conv_b[l]), row(conv_ln_g[l]), row(conv_ln_b[l]), B, S, width)

        merged = _mm_merge(hb, o_r, o_c, w[:, c_gate:c_gate + D].astype(BF16),
                           w[:, c_gate + D:].astype(BF16),
                           proj_rwkv[l].astype(BF16), proj_conv[l].astype(BF16))
        h, hb = _mm_res_ln(merged, w_out[l], h, row(ln1_g[l]), row(ln1_b[l]), alpha)

        kx = _mm_plain(mem_n, xattn_wk[l].astype(BF16), BF16)
        vx = _mm_plain(mem_n, xattn_wv[l].astype(BF16), BF16)
        qk = _qk_fold(xattn_wq[l].astype(BF16), kx, B, mem_len, XATTN_HEADS)
        vo = _vo_fold(vx, xattn_wo[l].astype(BF16), B, mem_len, XATTN_HEADS)
        h, hb = _attn_ln(hb, qk, vo, h, row(ln2_g[l]), row(ln2_b[l]), alpha, B, S, XATTN_HEADS)

        h = _mlp_ln(hb, mlp_w1[l], mlp_w2[l], h, row(ln3_g[l]), row(ln3_b[l]), alpha)
        hb = h.astype(BF16)
    return h.reshape(B, S, D)
```

```python
import functools

import jax
import jax.numpy as jnp
from jax import lax
from jax.experimental import pallas as pl
from jax.experimental.pallas import tpu as pltpu

F32 = jnp.float32
BF16 = jnp.bfloat16

LANES = 128
RWKV_HEAD = 64
PAIR = 2 * RWKV_HEAD
CHUNK = 64
CONV_HALO = 32
XATTN_HEADS = 4
LN_EPS = 1e-5
GN_EPS = 64e-5
VMEM_LIMIT = 56 * 1024 * 1024

_NN = (((1,), (0,)), ((), ()))
_NT = (((1,), (1,)), ((), ()))
_TN = (((0,), (0,)), ((), ()))
_DONE = object()


def _params(*sem):
    return pltpu.CompilerParams(dimension_semantics=sem, vmem_limit_bytes=VMEM_LIMIT)


def _dot(a, b, dims=_NN):
    return lax.dot_general(a, b, dims, preferred_element_type=F32)


def _sigmoid(x):
    return 1.0 / (1.0 + jnp.exp(-x))


def _layer_norm_rows(v, g, b):
    mu = jnp.mean(v, axis=-1, keepdims=True)
    d = v - mu
    var = jnp.mean(d * d, axis=-1, keepdims=True)
    return d * lax.rsqrt(var + LN_EPS) * g + b


def _tile(n, pref):
    t = min(n, pref)
    assert n % t == 0, (n, pref)
    return t


def _mm_shift_kernel(x_ref, w_ref, mu_ref, o_ref, carry_ref, *, tiles_per_seq):
    m = pl.program_id(1)
    z = _dot(x_ref[...], w_ref[...])
    tm = z.shape[0]

    @pl.when(m % tiles_per_seq == 0)
    def _():
        carry_ref[...] = jnp.zeros_like(carry_ref)

    prev_row = carry_ref[0:1, :]
    row = lax.broadcasted_iota(jnp.int32, z.shape, 0)
    prev = jnp.where(row == 0, prev_row, pltpu.roll(z, 1, axis=0))
    o_ref[...] = z + (prev - z) * mu_ref[...]
    carry_ref[0:1, :] = z[tm - 1:tm, :]


def _mm_shift(x, w, mu, seq, tm=2048, tn=512):
    M, K = x.shape
    N = w.shape[1]
    tm = _tile(seq, tm)
    tn = _tile(N, tn)
    return pl.pallas_call(
        functools.partial(_mm_shift_kernel, tiles_per_seq=seq // tm),
        out_shape=jax.ShapeDtypeStruct((M, N), F32),
        grid=(N // tn, M // tm),
        in_specs=[
            pl.BlockSpec((tm, K), lambda n, m: (m, 0)),
            pl.BlockSpec((K, tn), lambda n, m: (0, n)),
            pl.BlockSpec((1, tn), lambda n, m: (0, n)),
        ],
        out_specs=pl.BlockSpec((tm, tn), lambda n, m: (m, n)),
        scratch_shapes=[pltpu.VMEM((8, tn), F32)],
        compiler_params=_params("parallel", "arbitrary"),
    )(x, w, mu)


def _mm_glu_kernel(x_ref, wa_ref, wb_ref, o_ref):
    x = x_ref[...]
    o_ref[...] = _dot(x, wa_ref[...]) * _sigmoid(_dot(x, wb_ref[...]))


def _mm_glu(x, wa, wb, tm=1024, tn=512):
    M, K = x.shape
    N = wa.shape[1]
    tm = _tile(M, tm)
    tn = _tile(N, tn)
    return pl.pallas_call(
        _mm_glu_kernel,
        out_shape=jax.ShapeDtypeStruct((M, N), F32),
        grid=(N // tn, M // tm),
        in_specs=[
            pl.BlockSpec((tm, K), lambda n, m: (m, 0)),
            pl.BlockSpec((K, tn), lambda n, m: (0, n)),
            pl.BlockSpec((K, tn), lambda n, m: (0, n)),
        ],
        out_specs=pl.BlockSpec((tm, tn), lambda n, m: (m, n)),
        compiler_params=_params("parallel", "arbitrary"),
    )(x, wa, wb)


def _mm_plain_kernel(x_ref, w_ref, o_ref):
    o_ref[...] = _dot(x_ref[...], w_ref[...]).astype(o_ref.dtype)


def _mm_plain(x, w, out_dtype, tm=1024, tn=512):
    M, K = x.shape
    N = w.shape[1]
    tm = _tile(M, tm)
    tn = _tile(N, tn)
    return pl.pallas_call(
        _mm_plain_kernel,
        out_shape=jax.ShapeDtypeStruct((M, N), out_dtype),
        grid=(N // tn, M // tm),
        in_specs=[
            pl.BlockSpec((tm, K), lambda n, m: (m, 0)),
            pl.BlockSpec((K, tn), lambda n, m: (0, n)),
        ],
        out_specs=pl.BlockSpec((tm, tn), lambda n, m: (m, n)),
        compiler_params=_params("parallel", "arbitrary"),
    )(x, w)


def _mm_merge_kernel(x_ref, orw_ref, ocv_ref, wgr_ref, wgc_ref, pr_ref, pc_ref, o_ref):
    x = x_ref[...]
    gate_r = _sigmoid(_dot(x, wgr_ref[...]))
    gate_c = _sigmoid(_dot(x, wgc_ref[...]))
    merged = gate_r * _dot(orw_ref[...], pr_ref[...]) + gate_c * _dot(ocv_ref[...], pc_ref[...])
    o_ref[...] = merged.astype(o_ref.dtype)


def _mm_merge(x, o_r, o_c, wgr, wgc, p_r, p_c, tm=512, tn=512):
    M, K = x.shape
    Kb = o_r.shape[1]
    N = wgr.shape[1]
    tm = _tile(M, tm)
    tn = _tile(N, tn)
    row = lambda k: pl.BlockSpec((tm, k), lambda n, m: (m, 0))
    col = lambda k: pl.BlockSpec((k, tn), lambda n, m: (0, n))
    return pl.pallas_call(
        _mm_merge_kernel,
        out_shape=jax.ShapeDtypeStruct((M, N), BF16),
        grid=(N // tn, M // tm),
        in_specs=[row(K), row(Kb), row(Kb), col(K), col(K), col(Kb), col(Kb)],
        out_specs=pl.BlockSpec((tm, tn), lambda n, m: (m, n)),
        compiler_params=_params("parallel", "arbitrary"),
    )(x, o_r, o_c, wgr, wgc, p_r, p_c)


def _ln_inplace(o_ref, ob_ref, g_ref, b_ref, rows=128):
    g = g_ref[...]
    b = b_ref[...]

    def body(i, _):
        sl = pl.ds(pl.multiple_of(i * rows, rows), rows)
        y = _layer_norm_rows(o_ref[sl, :], g, b)
        o_ref[sl, :] = y
        if ob_ref is not None:
            ob_ref[sl, :] = y.astype(ob_ref.dtype)
        return 0

    lax.fori_loop(0, o_ref.shape[0] // rows, body, 0)


def _mm_res_ln_kernel(y_ref, w_ref, res_ref, g_ref, b_ref, o_ref, ob_ref, *, alpha, tn):
    n = pl.program_id(1)
    col = pl.ds(pl.multiple_of(n * tn, tn), tn)
    o_ref[:, col] = alpha * res_ref[...] + _dot(y_ref[...], w_ref[...])

    @pl.when(n == pl.num_programs(1) - 1)
    def _():
        _ln_inplace(o_ref, ob_ref, g_ref, b_ref)


def _mm_res_ln(y, w, res, g, b, alpha, tm=512, tn=512):
    M, K = y.shape
    N = w.shape[1]
    tm = _tile(M, tm)
    tn = _tile(N, tn)
    return pl.pallas_call(
        functools.partial(_mm_res_ln_kernel, alpha=alpha, tn=tn),
        out_shape=(jax.ShapeDtypeStruct((M, N), F32), jax.ShapeDtypeStruct((M, N), BF16)),
        grid=(M // tm, N // tn),
        in_specs=[
            pl.BlockSpec((tm, K), lambda m, n: (m, 0)),
            pl.BlockSpec((K, tn), lambda m, n: (0, n)),
            pl.BlockSpec((tm, tn), lambda m, n: (m, n)),
            pl.BlockSpec((1, N), lambda m, n: (0, 0)),
            pl.BlockSpec((1, N), lambda m, n: (0, 0)),
        ],
        out_specs=(pl.BlockSpec((tm, N), lambda m, n: (m, 0)),
                   pl.BlockSpec((tm, N), lambda m, n: (m, 0))),
        compiler_params=_params("parallel", "arbitrary"),
    )(y, w.astype(BF16), res, g, b)


def _mlp_ln_kernel(x_ref, w1_ref, w2_ref, res_ref, g_ref, b_ref, o_ref, *, alpha, tn):
    k = pl.program_id(1)
    h = jnp.maximum(_dot(x_ref[...], w1_ref[...]), 0.0)
    h = (h * h).astype(BF16)
    n_slabs = o_ref.shape[1] // tn

    @pl.when(k == 0)
    def _():
        for j in range(n_slabs):
            sl = slice(j * tn, (j + 1) * tn)
            o_ref[:, sl] = _dot(h, w2_ref[:, sl])

    @pl.when(k > 0)
    def _():
        for j in range(n_slabs):
            sl = slice(j * tn, (j + 1) * tn)
            o_ref[:, sl] += _dot(h, w2_ref[:, sl])

    @pl.when(k < n_slabs)
    def _():
        col = pl.ds(pl.multiple_of(k * tn, tn), tn)
        o_ref[:, col] += alpha * res_ref[...]

    @pl.when(k == pl.num_programs(1) - 1)
    def _():
        _ln_inplace(o_ref, None, g_ref, b_ref)


def _mlp_ln(x, w1, w2, res, g, b, alpha, tm=1024, tk=512, tn=512):
    M, D = x.shape
    FF = w1.shape[1]
    tm = _tile(M, tm)
    tk = _tile(FF, tk)
    tn = _tile(D, tn)
    n_slabs = D // tn
    assert FF // tk >= n_slabs
    single = pl.Buffered(1)
    return pl.pallas_call(
        functools.partial(_mlp_ln_kernel, alpha=alpha, tn=tn),
        out_shape=jax.ShapeDtypeStruct((M, D), F32),
        grid=(M // tm, FF // tk),
        in_specs=[
            pl.BlockSpec((tm, D), lambda m, k: (m, 0), pipeline_mode=single),
            pl.BlockSpec((D, tk), lambda m, k: (0, k)),
            pl.BlockSpec((tk, D), lambda m, k: (k, 0)),
            pl.BlockSpec((tm, tn), lambda m, k: (m, jnp.minimum(k, n_slabs - 1))),
            pl.BlockSpec((1, D), lambda m, k: (0, 0)),
            pl.BlockSpec((1, D), lambda m, k: (0, 0)),
        ],
        out_specs=pl.BlockSpec((tm, D), lambda m, k: (m, 0), pipeline_mode=single),
        compiler_params=_params("parallel", "arbitrary"),
    )(x, w1.astype(BF16), w2.astype(BF16), res, g, b)


def _ln_rows_kernel(x_ref, g_ref, b_ref, o_ref):
    o_ref[...] = _layer_norm_rows(x_ref[...], g_ref[...], b_ref[...]).astype(o_ref.dtype)


def _ln_rows(x, g, b, out_dtype, tm=128):
    M, D = x.shape
    tm = _tile(M, tm)
    return pl.pallas_call(
        _ln_rows_kernel,
        out_shape=jax.ShapeDtypeStruct((M, D), out_dtype),
        grid=(M // tm,),
        in_specs=[pl.BlockSpec((tm, D), lambda m: (m, 0)),
                  pl.BlockSpec((1, D), lambda m: (0, 0)),
                  pl.BlockSpec((1, D), lambda m: (0, 0))],
        out_specs=pl.BlockSpec((tm, D), lambda m: (m, 0)),
        compiler_params=_params("parallel"),
    )(x, g, b)


def _lora_act_kernel(z_ref, o_ref):
    o_ref[:, 0:LANES] = jnp.tanh(z_ref[:, 0:LANES])
    o_ref[:, LANES:2 * LANES] = z_ref[:, LANES:2 * LANES]
    o_ref[:, 2 * LANES:] = _sigmoid(z_ref[:, 2 * LANES:])


def _lora_act(zs, col_block, tm=1024):
    M = zs.shape[0]
    W = 4 * LANES
    tm = _tile(M, tm)
    return pl.pallas_call(
        _lora_act_kernel,
        out_shape=jax.ShapeDtypeStruct((M, W), F32),
        grid=(M // tm,),
        in_specs=[pl.BlockSpec((tm, W), lambda m: (m, col_block))],
        out_specs=pl.BlockSpec((tm, W), lambda m: (m, 0)),
        compiler_params=_params("parallel"),
    )(zs)


def _rwkv_pair_stages(q, zr_ref, zk_ref, zv_ref, l_ref, wup_ref, aup_ref, gup_ref, prm_ref, o_ref, h_ref,
                      n_chunks):
    C = CHUNK
    lane = lax.broadcasted_iota(jnp.int32, (1, PAIR), 1)
    m0 = (lane < RWKV_HEAD).astype(F32)
    m1 = 1.0 - m0
    ri = lax.broadcasted_iota(jnp.int32, (PAIR, PAIR), 0)
    ci = lax.broadcasted_iota(jnp.int32, (PAIR, PAIR), 1)
    strict = ci < ri
    lower = ci <= ri
    head_bits = RWKV_HEAD.bit_length() - 1
    same_head = jnp.right_shift(ri, head_bits) == jnp.right_shift(ci, head_bits)
    eye = (ri == ci).astype(F32)
    head_ones = same_head.astype(BF16)
    rc = lax.broadcasted_iota(jnp.int32, (C, C), 0)
    cc = lax.broadcasted_iota(jnp.int32, (C, C), 1)
    tril_c = (cc <= rc).astype(BF16)
    assert n_chunks % 2 == 0

    lanes = slice(q * PAIR, (q + 1) * PAIR)
    prm = prm_ref[:, lanes]
    w0, a0, k_k, k_a, r_k, gn_g, gn_b = (prm[i:i + 1, :] for i in range(7))
    w_up = wup_ref[:, lanes].astype(BF16)
    a_up = aup_ref[:, lanes].astype(BF16)
    g_up = gup_ref[:, lanes].astype(BF16)

    def stack(x):
        return jnp.concatenate([x * m0, x * m1], axis=0)

    def head_sum(x, passes):
        acc = None
        for _ in range(passes):
            piece = x.astype(BF16)
            part = _dot(piece, head_ones)
            acc = part if acc is None else acc + part
            x = x - piece.astype(F32)
        return acc

    def bdot(a, b, dims=_NN):
        return _dot(a.astype(BF16), b.astype(BF16), dims)

    r = zr_ref[:, lanes]
    k = zk_ref[:, lanes]
    v = zv_ref[:, lanes]
    lora = l_ref[...].astype(BF16)
    w_raw = w0 + _dot(lora[:, 0:LANES], w_up)
    a_raw = a0 + _dot(lora[:, LANES:2 * LANES], a_up)
    g = _dot(lora[:, 2 * LANES:], g_up)
    w = -(jnp.maximum(-w_raw, 0.0) + jnp.log(1.0 + jnp.exp(-jnp.abs(w_raw)))) - 0.5
    wl = -jnp.exp(w)
    a_sig = _sigmoid(a_raw)
    kk = k * k_k
    kk = kk / jnp.maximum(jnp.sqrt(head_sum(kk * kk, 1)), 1e-12)
    k2 = k * (1.0 + (a_sig - 1.0) * k_a)
    b_s = kk * a_sig
    yield

    chunks = range(n_chunks)
    rows = [slice(c * C, (c + 1) * C) for c in chunks]
    wl_hi = wl.astype(BF16)
    wl_r = wl - wl_hi.astype(F32)
    wl_mid = wl_r.astype(BF16)
    wl_lo = (wl_r - wl_mid.astype(F32)).astype(BF16)
    wl_parts = jnp.concatenate([wl_hi, wl_mid, wl_lo], axis=1)
    cum_parts = [_dot(tril_c, wl_parts[s]) for s in rows]
    cum_c = [m[:, :PAIR] + (m[:, PAIR:2 * PAIR] + m[:, 2 * PAIR:]) for m in cum_parts]
    cum = jnp.concatenate(cum_c, axis=0)
    cum_end = jnp.concatenate([jnp.broadcast_to(m[C - 1:C, :], (C, PAIR)) for m in cum_c], axis=0)
    e_neg = jnp.exp(-cum)
    e_tail = jnp.exp(cum_end - cum)
    r_t = r * jnp.exp(cum)
    a_t = -kk * jnp.exp(cum - wl)
    b_t = b_s * e_neg
    k_t = k2 * e_neg
    bw = b_s * e_tail
    kw = k2 * e_tail
    w_end = jnp.exp(cum_end)
    yield

    a_st = [stack(a_t[s]).astype(BF16) for s in rows]
    r_st = [stack(r_t[s]) for s in rows]
    v_st = [stack(v[s]).astype(BF16) for s in rows]
    gram = [bdot(jnp.concatenate([a_st[c], r_st[c].astype(BF16)], axis=0),
                 jnp.concatenate([stack(b_t[rows[c]]), stack(k_t[rows[c]])], axis=0), _NT)
            for c in chunks]
    yield
    a_ab =[jnp.where(strict, m[:PAIR, :PAIR], 0.0) for m in gram]
    a_ak = [jnp.where(strict, m[:PAIR, PAIR:], 0.0).astype(BF16) for m in gram]
    g_rb = [jnp.where(lower, m[PAIR:, :PAIR], 0.0).astype(BF16) for m in gram]
    g_rk = [jnp.where(lower, m[PAIR:, PAIR:], 0.0).astype(BF16) for m in gram]

    pw = [m.astype(BF16) for m in a_ab]
    pw = [_dot(m, m).astype(BF16) for m in pw]
    t_inv = [eye + m for m in a_ab]
    yield
    n = 2
    while 2 * n < C:
        both = [_dot(pw[c], jnp.concatenate([pw[c], t_inv[c].astype(BF16)], axis=1)) for c in chunks]
        pw = [m[:, :PAIR].astype(BF16) for m in both]
        t_inv = [t_inv[c] + both[c][:, PAIR:] for c in chunks]
        n *= 2
        yield
    t_b = [(t_inv[c] + _dot(pw[c], t_inv[c].astype(BF16))).astype(BF16) for c in chunks]

    a_v = [_dot(a_ak[c], v_st[c]).astype(BF16) for c in chunks]
    yield
    t_av =[_dot(t_b[c], jnp.concatenate([a_st[c], a_v[c]], axis=1)).astype(BF16)
            for c in chunks]
    zero = jnp.zeros((PAIR, PAIR), BF16)
    rhs2 = [jnp.concatenate([t_av[c], jnp.concatenate([zero, v_st[c]], axis=1)], axis=0)
            for c in chunks]
    yield
    bw_st = [stack(bw[s]).astype(BF16) for s in rows]
    kw_st = [stack(kw[s]).astype(BF16) for s in rows]
    mn = [_dot(jnp.concatenate([bw_st[c], kw_st[c]], axis=0), rhs2[c], _TN) for c in chunks]
    qz = [_dot(jnp.concatenate([g_rb[c], g_rk[c]], axis=1), rhs2[c]) for c in chunks]
    m_c = [(eye * w_end[rows[c]][0:1, :] + mn[c][:, :PAIR]).astype(BF16) for c in chunks]
    n_c = [mn[c][:, PAIR:] for c in chunks]
    q_c = [(r_st[c] + qz[c][:, :PAIR]).astype(BF16) for c in chunks]
    z_c = [qz[c][:, PAIR:] for c in chunks]
    yield

    pairs = range(n_chunks // 2)
    comp = [_dot(m_c[2 * j + 1], jnp.concatenate([m_c[2 * j], n_c[2 * j].astype(BF16)], axis=1))
            for j in pairs]
    m_2 = [comp[j][:, :PAIR].astype(BF16) for j in pairs]
    n_2 = [comp[j][:, PAIR:] + n_c[2 * j + 1] for j in pairs]
    yield
    h = h_ref[q]
    outs = []
    for j in pairs:
        c0 = 2 * j
        c1 = c0 + 1
        h_b = h.astype(BF16)
        from_h = _dot(jnp.concatenate([m_2[j], m_c[c0], q_c[c0]], axis=0), h_b)
        h = from_h[:PAIR, :] + n_2[j]
        h_mid = (from_h[PAIR:2 * PAIR, :] + n_c[c0]).astype(BF16)
        o_0 = from_h[2 * PAIR:, :] + z_c[c0]
        o_1 = _dot(q_c[c1], h_mid) + z_c[c1]
        outs.append(o_0[:C, :] + o_0[C:, :])
        outs.append(o_1[:C, :] + o_1[C:, :])
        yield
    h_ref[q] = h
    o = jnp.concatenate(outs, axis=0)

    inv_n = 1.0 / RWKV_HEAD
    mu = head_sum(o, 2) * inv_n
    d = o - mu
    var = head_sum(d * d, 2) * inv_n
    y = d * lax.rsqrt(var + GN_EPS) * gn_g + gn_b
    y = y + head_sum(r * k2 * r_k, 1) * v
    o_ref[:, lanes] = (y * g).astype(o_ref.dtype)


def _rwkv_kernel(*refs, n_chunks):
    o_ref, h_ref = refs[-2:]

    @pl.when(pl.program_id(2) == 0)
    def _():
        h_ref[...] = jnp.zeros_like(h_ref)

    live = []
    for q in range(o_ref.shape[1] // PAIR):
        live.append(_rwkv_pair_stages(q, *refs, n_chunks))
        for _ in range(3):
            live = [gen for gen in live if next(gen, _DONE) is not _DONE]
    while live:
        live = [gen for gen in live if next(gen, _DONE) is not _DONE]


def _rwkv_mix(zs, lora, w_up, a_up, g_up, prm, batch, seq, d_rwkv, tt=1024, pairs_per_step=2):
    M = zs.shape[0]
    wb = pairs_per_step * PAIR
    n_cols = d_rwkv // wb
    tt = _tile(seq, tt)
    nt = seq // tt
    zspec = lambda off: pl.BlockSpec((tt, wb), lambda b, p, t: (b * nt + t, off + p))
    wspec = lambda k: pl.BlockSpec((k, wb), lambda b, p, t: (0, p))
    return pl.pallas_call(
        functools.partial(_rwkv_kernel, n_chunks=tt // CHUNK),
        out_shape=jax.ShapeDtypeStruct((M, d_rwkv), BF16),
        grid=(batch, n_cols, nt),
        in_specs=[
            zspec(0), zspec(n_cols), zspec(2 * n_cols),
            pl.BlockSpec((tt, 4 * LANES), lambda b, p, t: (b * nt + t, 0)),
            wspec(LANES), wspec(LANES), wspec(2 * LANES), wspec(8),
        ],
        out_specs=pl.BlockSpec((tt, wb), lambda b, p, t: (b * nt + t, p)),
        scratch_shapes=[pltpu.VMEM((pairs_per_step, PAIR, PAIR), F32)],
        compiler_params=_params("parallel", "parallel", "arbitrary"),
    )(zs, zs, zs, lora, w_up, a_up, g_up, prm)


def _conv_kernel(u_ref, w_ref, cb_ref, g_ref, b_ref, o_ref, buf_ref, acc_ref, win_ref, *, width):
    tt, ch = u_ref.shape
    t = pl.program_id(1)

    @pl.when(t == 0)
    def _():
        buf_ref[0:CONV_HALO, :] = jnp.zeros((CONV_HALO, ch), F32)

    buf_ref[CONV_HALO:, :] = u_ref[...]
    base = CONV_HALO - (width - 1)

    sub = 8

    def slab(s, _):
        cols = pl.ds(pl.multiple_of(s * LANES, LANES), LANES)
        acc = jnp.broadcast_to(cb_ref[:, cols], (tt, LANES))
        for res in range(sub):
            taps = [j for j in range(width) if (base + j) % sub == res]
            if not taps:
                continue
            span = (base + taps[-1]) // sub * sub
            win_ref[0:tt + span, :] = buf_ref[pl.ds(res, tt + span), cols]
            for j in taps:
                off = base + j - res
                acc = acc + w_ref[pl.ds(j, 1), cols] * win_ref[off:off + tt, :]
        acc_ref[:, cols] = acc
        return 0

    lax.fori_loop(0, ch // LANES, slab, 0)
    buf_ref[0:CONV_HALO, :] = buf_ref[tt:tt + CONV_HALO, :]

    g = g_ref[...]
    b = b_ref[...]
    rows = min(tt, 128)

    def norm(i, _):
        sl = pl.ds(pl.multiple_of(i * rows, rows), rows)
        y = _layer_norm_rows(acc_ref[sl, :], g, b)
        o_ref[sl, :] = (y * _sigmoid(y)).astype(o_ref.dtype)
        return 0

    lax.fori_loop(0, tt // rows, norm, 0)


def _conv_module(u, conv_w, conv_b, ln_g, ln_b, batch, seq, width, tt=256):
    M, ch = u.shape
    tt = _tile(seq, tt)
    nt = seq // tt
    vec = lambda r: pl.BlockSpec((r, ch), lambda b, t: (0, 0))
    return pl.pallas_call(
        functools.partial(_conv_kernel, width=width),
        out_shape=jax.ShapeDtypeStruct((M, ch), BF16),
        grid=(batch, nt),
        in_specs=[pl.BlockSpec((tt, ch), lambda b, t: (b * nt + t, 0)),
                  vec(conv_w.shape[0]), vec(1), vec(1), vec(1)],
        out_specs=pl.BlockSpec((tt, ch), lambda b, t: (b * nt + t, 0)),
        scratch_shapes=[pltpu.VMEM((tt + CONV_HALO, ch), F32), pltpu.VMEM((tt, ch), F32),
                        pltpu.VMEM((tt + CONV_HALO, LANES), F32)],
        compiler_params=_params("parallel", "arbitrary"),
    )(u, conv_w, conv_b, ln_g, ln_b)


def _qk_fold_kernel(wq_ref, k_ref, o_ref):
    o_ref[...] = _dot(wq_ref[...], k_ref[...], _NT).astype(o_ref.dtype)


def _qk_fold(wq, kx, batch, mem_len, heads):
    D = wq.shape[0]
    hd = D // heads
    return pl.pallas_call(
        _qk_fold_kernel,
        out_shape=jax.ShapeDtypeStruct((batch, D, heads * mem_len), BF16),
        grid=(heads, batch),
        in_specs=[pl.BlockSpec((D, hd), lambda h, b: (0, h)),
                  pl.BlockSpec((mem_len, hd), lambda h, b: (b, h))],
        out_specs=pl.BlockSpec((None, D, mem_len), lambda h, b: (b, 0, h)),
        compiler_params=_params("parallel", "arbitrary"),
    )(wq, kx)


def _vo_fold_kernel(v_ref, wo_ref, o_ref):
    o_ref[...] = _dot(v_ref[...], wo_ref[...]).astype(o_ref.dtype)


def _vo_fold(vx, wo, batch, mem_len, heads):
    D = wo.shape[0]
    hd = D // heads
    return pl.pallas_call(
        _vo_fold_kernel,
        out_shape=jax.ShapeDtypeStruct((batch, heads * mem_len, D), BF16),
        grid=(heads, batch),
        in_specs=[pl.BlockSpec((mem_len, hd), lambda h, b: (b, h)),
                  pl.BlockSpec((hd, D), lambda h, b: (h, 0))],
        out_specs=pl.BlockSpec((None, mem_len, D), lambda h, b: (b, h, 0)),
        compiler_params=_params("parallel", "arbitrary"),
    )(vx, wo)


def _attn_ln_kernel(x_ref, qk_ref, vo_ref, res_ref, g_ref, b_ref, o_ref, ob_ref, *, heads, scale, alpha):
    s = _dot(x_ref[...], qk_ref[...]) * scale
    mem_len = s.shape[1] // heads
    probs = []
    for h in range(heads):
        sh = s[:, h * mem_len:(h + 1) * mem_len]
        e = jnp.exp(sh - jnp.max(sh, axis=-1, keepdims=True))
        probs.append((e / jnp.sum(e, axis=-1, keepdims=True)).astype(BF16))
    p = jnp.concatenate(probs, axis=1)
    o_ref[...] = alpha * res_ref[...] + _dot(p, vo_ref[...])
    _ln_inplace(o_ref, ob_ref, g_ref, b_ref)


def _attn_ln(x, qk, vo, res, g, b, alpha, batch, seq, heads, tq=256):
    M, D = x.shape
    hm = qk.shape[2]
    tq = _tile(seq, tq)
    nt = seq // tq
    scale = float((D // heads) ** -0.5)
    single = pl.Buffered(1)
    tok = lambda: pl.BlockSpec((tq, D), lambda bi, t: (bi * nt + t, 0))
    vec = pl.BlockSpec((1, D), lambda bi, t: (0, 0))
    return pl.pallas_call(
        functools.partial(_attn_ln_kernel, heads=heads, scale=scale, alpha=alpha),
        out_shape=(jax.ShapeDtypeStruct((M, D), F32), jax.ShapeDtypeStruct((M, D), BF16)),
        grid=(batch, nt),
        in_specs=[tok(),
                  pl.BlockSpec((None, D, hm), lambda bi, t: (bi, 0, 0), pipeline_mode=single),
                  pl.BlockSpec((None, hm, D), lambda bi, t: (bi, 0, 0), pipeline_mode=single),
                  tok(), vec, vec],
        out_specs=(tok(), tok()),
        compiler_params=_params("parallel", "arbitrary"),
    )(x, qk, vo, res, g, b)


def _pad_cols(w, n):
    return jnp.pad(w, ((0, 0), (0, n - w.shape[1])))


def _pad_rows(w, n):
    return jnp.pad(w, ((0, n - w.shape[0]), (0, 0)))


def kernel(x, mem, w_in, rwkv_shift_mix, rwkv_w0, rwkv_w_up, rwkv_a0, rwkv_a_up, rwkv_g_up, rwkv_k_k, rwkv_k_a, rwkv_r_k, rwkv_gn_g, rwkv_gn_b, conv_w, conv_b, conv_ln_g, conv_ln_b, proj_rwkv, proj_conv, w_out, ln1_g, ln1_b, ln_mem_g, ln_mem_b, xattn_wq, xattn_wk, xattn_wv, xattn_wo, ln2_g, ln2_b, mlp_w1, mlp_w2, ln3_g, ln3_b):
    B, S, D = x.shape
    depth = w_in.shape[0]
    mem_len = mem.shape[1]
    d_rwkv = rwkv_w0.shape[1]
    d_conv = conv_b.shape[1]
    r_decay = rwkv_w_up.shape[1]
    r_iclr = rwkv_a_up.shape[1]
    r_gate = rwkv_g_up.shape[1]
    width = conv_w.shape[1]
    assert r_decay <= LANES and r_iclr <= LANES and r_gate == 2 * LANES
    assert d_rwkv % PAIR == 0 and width - 1 <= CONV_HALO
    alpha = float((2 * depth) ** 0.25)
    M = B * S
    row = lambda a: a.reshape(1, -1)

    mem_n = _ln_rows(mem.reshape(B * mem_len, D), row(ln_mem_g), row(ln_mem_b), BF16)
    h = x.reshape(M, D)
    hb = h.astype(BF16)
    for l in range(depth):
        w = w_in[l]
        c_lora = 3 * d_rwkv
        c_conv = c_lora + r_decay + r_iclr + r_gate
        c_gate = c_conv + 2 * d_conv
        mix = rwkv_shift_mix[l]
        zs = _mm_shift(hb, w[:, :c_lora].astype(BF16), row(mix[:c_lora]), S)
        w_lora = jnp.concatenate([
            _pad_cols(w[:, c_lora:c_lora + r_decay], LANES),
            _pad_cols(w[:, c_lora + r_decay:c_lora + r_decay + r_iclr], LANES),
            w[:, c_lora + r_decay + r_iclr:c_conv]], axis=1).astype(BF16)
        mu_lora = jnp.concatenate([
            jnp.pad(mix[c_lora:c_lora + r_decay], (0, LANES - r_decay)),
            jnp.pad(mix[c_lora + r_decay:c_lora + r_decay + r_iclr], (0, LANES - r_iclr)),
            mix[c_lora + r_decay + r_iclr:]]).reshape(1, -1)
        lora = _lora_act(_mm_shift(hb, w_lora, mu_lora, S), 0)
        prm = jnp.stack([rwkv_w0[l], rwkv_a0[l], rwkv_k_k[l], rwkv_k_a[l], rwkv_r_k[l].reshape(-1),
                         rwkv_gn_g[l], rwkv_gn_b[l], jnp.zeros((d_rwkv,), F32)])
        o_r = _rwkv_mix(zs, lora, _pad_rows(rwkv_w_up[l], LANES), _pad_rows(rwkv_a_up[l], LANES),
                        rwkv_g_up[l], prm, B, S, d_rwkv)

        u = _mm_glu(hb, w[:, c_conv:c_conv + d_conv].astype(BF16),
                    w[:, c_conv + d_conv:c_gate].astype(BF16))
        cw = _pad_rows(conv_w[l].reshape(width, d_conv), CONV_HALO)
        o_c = _conv_module(u, cw, row(conv_b[l]), row(conv_ln_g[l]), row(conv_ln_b[l]), B, S, width)

        merged = _mm_merge(hb, o_r, o_c, w[:, c_gate:c_gate + D].astype(BF16),
                           w[:, c_gate + D:].astype(BF16),
                           proj_rwkv[l].astype(BF16), proj_conv[l].astype(BF16))
        h, hb = _mm_res_ln(merged, w_out[l], h, row(ln1_g[l]), row(ln1_b[l]), alpha)

        kx = _mm_plain(mem_n, xattn_wk[l].astype(BF16), BF16)
        vx = _mm_plain(mem_n, xattn_wv[l].astype(BF16), BF16)
        qk = _qk_fold(xattn_wq[l].astype(BF16), kx, B, mem_len, XATTN_HEADS)
        vo = _vo_fold(vx, xattn_wo[l].astype(BF16), B, mem_len, XATTN_HEADS)
        h, hb = _attn_ln(hb, qk, vo, h, row(ln2_g[l]), row(ln2_b[l]), alpha, B, S, XATTN_HEADS)

        h = _mlp_ln(hb, mlp_w1[l], mlp_w2[l], h, row(ln3_g[l]), row(ln3_b[l]), alpha)
        hb = h.astype(BF16)
    return h.reshape(B, S, D)
```

```python
import functools
import math

import jax
import jax.numpy as jnp
from jax import lax
from jax.experimental import pallas as pl
from jax.experimental.pallas import tpu as pltpu

F32 = jnp.float32
BF16 = jnp.bfloat16

LANES = 128
RWKV_HEAD = 64
PAIR = 2 * RWKV_HEAD
CHUNK = 64
CONV_HALO = 32
XATTN_HEADS = 4
LN_EPS = 1e-5
GN_EPS = 64e-5
VMEM_LIMIT = 56 * 1024 * 1024

_NN = (((1,), (0,)), ((), ()))
_NT = (((1,), (1,)), ((), ()))
_TN = (((0,), (0,)), ((), ()))
_DONE = object()


def _params(*sem):
    return pltpu.CompilerParams(dimension_semantics=sem, vmem_limit_bytes=VMEM_LIMIT)


def _dot(a, b, dims=_NN):
    return lax.dot_general(a, b, dims, preferred_element_type=F32)


def _sigmoid(x):
    return 1.0 / (1.0 + jnp.exp(-x))


def _layer_norm_rows(v, g, b):
    mu = jnp.mean(v, axis=-1, keepdims=True)
    d = v - mu
    var = jnp.mean(d * d, axis=-1, keepdims=True)
    return d * lax.rsqrt(var + LN_EPS) * g + b


def _tile(n, pref):
    t = min(n, pref)
    assert n % t == 0, (n, pref)
    return t


def _mm_shift_kernel(x_ref, w_ref, mu_ref, o_ref, carry_ref, *, tiles_per_seq):
    m = pl.program_id(1)
    z = _dot(x_ref[...], w_ref[...])
    tm = z.shape[0]

    @pl.when(m % tiles_per_seq == 0)
    def _():
        carry_ref[...] = jnp.zeros_like(carry_ref)

    prev_row = carry_ref[0:1, :]
    row = lax.broadcasted_iota(jnp.int32, z.shape, 0)
    prev = jnp.where(row == 0, prev_row, pltpu.roll(z, 1, axis=0))
    o_ref[...] = z + (prev - z) * mu_ref[...]
    carry_ref[0:1, :] = z[tm - 1:tm, :]


W_COL_TILE = 512


def _mm_shift(x, w, mu, seq, tm=2048, tn=W_COL_TILE):
    M, K = x.shape
    N = mu.shape[1]
    tm = _tile(seq, tm)
    tn = _tile(N, tn)
    return pl.pallas_call(
        functools.partial(_mm_shift_kernel, tiles_per_seq=seq // tm),
        out_shape=jax.ShapeDtypeStruct((M, N), F32),
        grid=(N // tn, M // tm),
        in_specs=[
            pl.BlockSpec((tm, K), lambda n, m: (m, 0)),
            pl.BlockSpec((K, tn), lambda n, m: (0, n)),
            pl.BlockSpec((1, tn), lambda n, m: (0, n)),
        ],
        out_specs=pl.BlockSpec((tm, tn), lambda n, m: (m, n)),
        scratch_shapes=[pltpu.VMEM((8, tn), F32)],
        compiler_params=_params("parallel", "arbitrary"),
    )(x, w, mu)


def _mm_glu_kernel(x_ref, wa_ref, wb_ref, o_ref):
    x = x_ref[...]
    o_ref[...] = _dot(x, wa_ref[...]) * _sigmoid(_dot(x, wb_ref[...]))


def _mm_glu(x, w, col_a, col_b, N, tm=1024, tn=W_COL_TILE):
    M, K = x.shape
    tm = _tile(M, tm)
    assert N % tn == 0 and col_a % tn == 0 and col_b % tn == 0
    ja, jb = col_a // tn, col_b // tn
    return pl.pallas_call(
        _mm_glu_kernel,
        out_shape=jax.ShapeDtypeStruct((M, N), F32),
        grid=(N // tn, M // tm),
        in_specs=[
            pl.BlockSpec((tm, K), lambda n, m: (m, 0)),
            pl.BlockSpec((K, tn), lambda n, m: (0, ja + n)),
            pl.BlockSpec((K, tn), lambda n, m: (0, jb + n)),
        ],
        out_specs=pl.BlockSpec((tm, tn), lambda n, m: (m, n)),
        compiler_params=_params("parallel", "arbitrary"),
    )(x, w, w)


def _mm_plain_kernel(x_ref, w_ref, o_ref):
    o_ref[...] = _dot(x_ref[...], w_ref[...]).astype(o_ref.dtype)


def _mm_plain(x, w, out_dtype, tm=1024, tn=512):
    M, K = x.shape
    N = w.shape[1]
    tm = _tile(M, tm)
    tn = _tile(N, tn)
    return pl.pallas_call(
        _mm_plain_kernel,
        out_shape=jax.ShapeDtypeStruct((M, N), out_dtype),
        grid=(N // tn, M // tm),
        in_specs=[
            pl.BlockSpec((tm, K), lambda n, m: (m, 0)),
            pl.BlockSpec((K, tn), lambda n, m: (0, n)),
        ],
        out_specs=pl.BlockSpec((tm, tn), lambda n, m: (m, n)),
        compiler_params=_params("parallel", "arbitrary"),
    )(x, w)


def _mm_merge_kernel(x_ref, orw_ref, ocv_ref, wgr_ref, wgc_ref, pr_ref, pc_ref, o_ref):
    x = x_ref[...]
    gate_r = _sigmoid(_dot(x, wgr_ref[...]))
    gate_c = _sigmoid(_dot(x, wgc_ref[...]))
    merged = gate_r * _dot(orw_ref[...], pr_ref[...]) + gate_c * _dot(ocv_ref[...], pc_ref[...])
    o_ref[...] = merged.astype(o_ref.dtype)


def _mm_merge(x, o_r, o_c, w, col_r, col_c, p_r, p_c, tm=512, tn=W_COL_TILE):
    M, K = x.shape
    Kb = o_r.shape[1]
    N = p_r.shape[1]
    tm = _tile(M, tm)
    assert N % tn == 0 and col_r % tn == 0 and col_c % tn == 0
    jr, jc = col_r // tn, col_c // tn
    row = lambda k: pl.BlockSpec((tm, k), lambda n, m: (m, 0))
    col = lambda k, j0: pl.BlockSpec((k, tn), lambda n, m: (0, j0 + n))
    return pl.pallas_call(
        _mm_merge_kernel,
        out_shape=jax.ShapeDtypeStruct((M, N), BF16),
        grid=(N // tn, M // tm),
        in_specs=[row(K), row(Kb), row(Kb), col(K, jr), col(K, jc), col(Kb, 0), col(Kb, 0)],
        out_specs=pl.BlockSpec((tm, tn), lambda n, m: (m, n)),
        compiler_params=_params("parallel", "arbitrary"),
    )(x, o_r, o_c, w, w, p_r, p_c)


def _ln_inplace(o_ref, ob_ref, g_ref, b_ref, rows=128):
    g = g_ref[...]
    b = b_ref[...]

    def body(i, _):
        sl = pl.ds(pl.multiple_of(i * rows, rows), rows)
        y = _layer_norm_rows(o_ref[sl, :], g, b)
        o_ref[sl, :] = y
        if ob_ref is not None:
            ob_ref[sl, :] = y.astype(ob_ref.dtype)
        return 0

    lax.fori_loop(0, o_ref.shape[0] // rows, body, 0)


def _mm_res_ln_kernel(y_ref, w_ref, res_ref, g_ref, b_ref, o_ref, ob_ref, *, alpha, tn):
    n = pl.program_id(1)
    col = pl.ds(pl.multiple_of(n * tn, tn), tn)
    o_ref[:, col] = alpha * res_ref[...] + _dot(y_ref[...], w_ref[...])

    @pl.when(n == pl.num_programs(1) - 1)
    def _():
        _ln_inplace(o_ref, ob_ref, g_ref, b_ref)


def _mm_res_ln(y, w, res, g, b, alpha, tm=512, tn=512):
    M, K = y.shape
    N = w.shape[1]
    tm = _tile(M, tm)
    tn = _tile(N, tn)
    return pl.pallas_call(
        functools.partial(_mm_res_ln_kernel, alpha=alpha, tn=tn),
        out_shape=(jax.ShapeDtypeStruct((M, N), F32), jax.ShapeDtypeStruct((M, N), BF16)),
        grid=(M // tm, N // tn),
        in_specs=[
            pl.BlockSpec((tm, K), lambda m, n: (m, 0)),
            pl.BlockSpec((K, tn), lambda m, n: (0, n)),
            pl.BlockSpec((tm, tn), lambda m, n: (m, n)),
            pl.BlockSpec((1, N), lambda m, n: (0, 0)),
            pl.BlockSpec((1, N), lambda m, n: (0, 0)),
        ],
        out_specs=(pl.BlockSpec((tm, N), lambda m, n: (m, 0)),
                   pl.BlockSpec((tm, N), lambda m, n: (m, 0))),
        compiler_params=_params("parallel", "arbitrary"),
    )(y, w.astype(BF16), res, g, b)


def _mlp_ln_kernel(x_ref, w1_ref, w2_ref, res_ref, g_ref, b_ref, o_ref, *, alpha, tn):
    k = pl.program_id(1)
    h = jnp.maximum(_dot(x_ref[...], w1_ref[...]), 0.0)
    h = (h * h).astype(BF16)
    n_slabs = o_ref.shape[1] // tn

    @pl.when(k == 0)
    def _():
        for j in range(n_slabs):
            sl = slice(j * tn, (j + 1) * tn)
            o_ref[:, sl] = _dot(h, w2_ref[:, sl])

    @pl.when(k > 0)
    def _():
        for j in range(n_slabs):
            sl = slice(j * tn, (j + 1) * tn)
            o_ref[:, sl] += _dot(h, w2_ref[:, sl])

    @pl.when(k < n_slabs)
    def _():
        col = pl.ds(pl.multiple_of(k * tn, tn), tn)
        o_ref[:, col] += alpha * res_ref[...]

    @pl.when(k == pl.num_programs(1) - 1)
    def _():
        _ln_inplace(o_ref, None, g_ref, b_ref)


def _mlp_ln(x, w1, w2, res, g, b, alpha, tm=1024, tk=512, tn=512):
    M, D = x.shape
    FF = w1.shape[1]
    tm = _tile(M, tm)
    tk = _tile(FF, tk)
    tn = _tile(D, tn)
    n_slabs = D // tn
    assert FF // tk >= n_slabs
    single = pl.Buffered(1)
    return pl.pallas_call(
        functools.partial(_mlp_ln_kernel, alpha=alpha, tn=tn),
        out_shape=jax.ShapeDtypeStruct((M, D), F32),
        grid=(M // tm, FF // tk),
        in_specs=[
            pl.BlockSpec((tm, D), lambda m, k: (m, 0), pipeline_mode=single),
            pl.BlockSpec((D, tk), lambda m, k: (0, k)),
            pl.BlockSpec((tk, D), lambda m, k: (k, 0)),
            pl.BlockSpec((tm, tn), lambda m, k: (m, jnp.minimum(k, n_slabs - 1))),
            pl.BlockSpec((1, D), lambda m, k: (0, 0)),
            pl.BlockSpec((1, D), lambda m, k: (0, 0)),
        ],
        out_specs=pl.BlockSpec((tm, D), lambda m, k: (m, 0), pipeline_mode=single),
        compiler_params=_params("parallel", "arbitrary"),
    )(x, w1.astype(BF16), w2.astype(BF16), res, g, b)


def _ln_rows_kernel(x_ref, g_ref, b_ref, o_ref):
    o_ref[...] = _layer_norm_rows(x_ref[...], g_ref[...], b_ref[...]).astype(o_ref.dtype)


def _ln_rows(x, g, b, out_dtype, tm=128):
    M, D = x.shape
    tm = _tile(M, tm)
    return pl.pallas_call(
        _ln_rows_kernel,
        out_shape=jax.ShapeDtypeStruct((M, D), out_dtype),
        grid=(M // tm,),
        in_specs=[pl.BlockSpec((tm, D), lambda m: (m, 0)),
                  pl.BlockSpec((1, D), lambda m: (0, 0)),
                  pl.BlockSpec((1, D), lambda m: (0, 0))],
        out_specs=pl.BlockSpec((tm, D), lambda m: (m, 0)),
        compiler_params=_params("parallel"),
    )(x, g, b)


def _lora_act_kernel(z_ref, o_ref):
    o_ref[:, 0:LANES] = jnp.tanh(z_ref[:, 0:LANES])
    o_ref[:, LANES:2 * LANES] = z_ref[:, LANES:2 * LANES]
    o_ref[:, 2 * LANES:] = _sigmoid(z_ref[:, 2 * LANES:])


def _lora_act(zs, col_block, tm=1024):
    M = zs.shape[0]
    W = 4 * LANES
    tm = _tile(M, tm)
    return pl.pallas_call(
        _lora_act_kernel,
        out_shape=jax.ShapeDtypeStruct((M, W), F32),
        grid=(M // tm,),
        in_specs=[pl.BlockSpec((tm, W), lambda m: (m, col_block))],
        out_specs=pl.BlockSpec((tm, W), lambda m: (m, 0)),
        compiler_params=_params("parallel"),
    )(zs)


def _rwkv_pair_stages(q, zr_ref, zk_ref, zv_ref, l_ref, wup_ref, aup_ref, gup_ref, prm_ref, ones_ref,
                      o_ref, h_ref, n_chunks):
    C = CHUNK
    lane = lax.broadcasted_iota(jnp.int32, (1, PAIR), 1)
    m0 = (lane < RWKV_HEAD).astype(F32)
    m1 = 1.0 - m0
    ri = lax.broadcasted_iota(jnp.int32, (PAIR, PAIR), 0)
    ci = lax.broadcasted_iota(jnp.int32, (PAIR, PAIR), 1)
    strict = ci < ri
    lower = ci <= ri
    eye = (ri == ci).astype(F32)
    head_ones = ones_ref[...]
    rc = lax.broadcasted_iota(jnp.int32, (C, C), 0)
    cc = lax.broadcasted_iota(jnp.int32, (C, C), 1)
    tril_c = (cc <= rc).astype(BF16)
    assert n_chunks % 2 == 0

    lanes = slice(q * PAIR, (q + 1) * PAIR)
    prm = prm_ref[:, lanes]
    w0, a0, k_k, k_a, r_k, gn_g, gn_b = (prm[i:i + 1, :] for i in range(7))
    w_up = wup_ref[:, lanes].astype(BF16)
    a_up = aup_ref[:, lanes].astype(BF16)
    g_up = gup_ref[:, lanes].astype(BF16)

    def stack(x):
        return jnp.concatenate([x * m0, x * m1], axis=0)

    def head_sum(x, passes):
        acc = None
        for _ in range(passes):
            piece = x.astype(BF16)
            part = _dot(piece, head_ones)
            acc = part if acc is None else acc + part
            x = x - piece.astype(F32)
        return acc

    def bdot(a, b, dims=_NN):
        return _dot(a.astype(BF16), b.astype(BF16), dims)

    r = zr_ref[:, lanes]
    k = zk_ref[:, lanes]
    v = zv_ref[:, lanes]
    lora = l_ref[...].astype(BF16)
    w_raw = w0 + _dot(lora[:, 0:LANES], w_up)
    a_raw = a0 + _dot(lora[:, LANES:2 * LANES], a_up)
    g = _dot(lora[:, 2 * LANES:], g_up)
    wl = -math.exp(-0.5) * _sigmoid(w_raw)
    a_sig = _sigmoid(a_raw)
    kk = k * k_k
    kk = kk * lax.rsqrt(jnp.maximum(head_sum(kk * kk, 1), 1e-24))
    k2 = k * (1.0 + (a_sig - 1.0) * k_a)
    b_s = kk * a_sig
    yield

    chunks = range(n_chunks)
    rows = [slice(c * C, (c + 1) * C) for c in chunks]
    wl_hi = wl.astype(BF16)
    wl_r = wl - wl_hi.astype(F32)
    wl_mid = wl_r.astype(BF16)
    wl_lo = (wl_r - wl_mid.astype(F32)).astype(BF16)
    wl_parts = jnp.concatenate([wl_hi, wl_mid, wl_lo], axis=1)
    cum_parts = [_dot(tril_c, wl_parts[s]) for s in rows]
    cum_c = [m[:, :PAIR] + (m[:, PAIR:2 * PAIR] + m[:, 2 * PAIR:]) for m in cum_parts]
    cum = jnp.concatenate(cum_c, axis=0)
    cum_end = jnp.concatenate([jnp.broadcast_to(m[C - 1:C, :], (C, PAIR)) for m in cum_c], axis=0)
    e_neg = jnp.exp(-cum)
    e_tail = jnp.exp(cum_end - cum)
    r_t = r * jnp.exp(cum)
    a_t = -kk * jnp.exp(cum - wl)
    b_t = b_s * e_neg
    k_t = k2 * e_neg
    bw = b_s * e_tail
    kw = k2 * e_tail
    w_end = jnp.exp(cum_end)
    yield

    a_st = [stack(a_t[s]).astype(BF16) for s in rows]
    r_st = [stack(r_t[s]) for s in rows]
    v_st = [stack(v[s]).astype(BF16) for s in rows]
    gram = [bdot(jnp.concatenate([a_st[c], r_st[c].astype(BF16)], axis=0),
                 jnp.concatenate([stack(b_t[rows[c]]), stack(k_t[rows[c]])], axis=0), _NT)
            for c in chunks]
    yield
    a_ab =[jnp.where(strict, m[:PAIR, :PAIR], 0.0) for m in gram]
    a_ak = [jnp.where(strict, m[:PAIR, PAIR:], 0.0).astype(BF16) for m in gram]
    g_rb = [jnp.where(lower, m[PAIR:, :PAIR], 0.0).astype(BF16) for m in gram]
    g_rk = [jnp.where(lower, m[PAIR:, PAIR:], 0.0).astype(BF16) for m in gram]

    pw = [m.astype(BF16) for m in a_ab]
    pw = [_dot(m, m).astype(BF16) for m in pw]
    t_inv = [eye + m for m in a_ab]
    yield
    n = 2
    while 2 * n < C:
        both = [_dot(pw[c], jnp.concatenate([pw[c], t_inv[c].astype(BF16)], axis=1)) for c in chunks]
        pw = [m[:, :PAIR].astype(BF16) for m in both]
        t_inv = [t_inv[c] + both[c][:, PAIR:] for c in chunks]
        n *= 2
        yield
    t_b = [(t_inv[c] + _dot(pw[c], t_inv[c].astype(BF16))).astype(BF16) for c in chunks]

    a_v = [_dot(a_ak[c], v_st[c]).astype(BF16) for c in chunks]
    yield
    t_av =[_dot(t_b[c], jnp.concatenate([a_st[c], a_v[c]], axis=1)).astype(BF16)
            for c in chunks]
    zero = jnp.zeros((PAIR, PAIR), BF16)
    rhs2 = [jnp.concatenate([t_av[c], jnp.concatenate([zero, v_st[c]], axis=1)], axis=0)
            for c in chunks]
    yield
    bw_st = [stack(bw[s]).astype(BF16) for s in rows]
    kw_st = [stack(kw[s]).astype(BF16) for s in rows]
    mn = [_dot(jnp.concatenate([bw_st[c], kw_st[c]], axis=0), rhs2[c], _TN) for c in chunks]
    qz = [_dot(jnp.concatenate([g_rb[c], g_rk[c]], axis=1), rhs2[c]) for c in chunks]
    m_c = [(eye * w_end[rows[c]][0:1, :] + mn[c][:, :PAIR]).astype(BF16) for c in chunks]
    n_c = [mn[c][:, PAIR:] for c in chunks]
    q_c = [(r_st[c] + qz[c][:, :PAIR]).astype(BF16) for c in chunks]
    z_c = [qz[c][:, PAIR:] for c in chunks]
    yield

    pairs = range(n_chunks // 2)
    comp = [_dot(m_c[2 * j + 1], jnp.concatenate([m_c[2 * j], n_c[2 * j].astype(BF16)], axis=1))
            for j in pairs]
    m_2 = [comp[j][:, :PAIR].astype(BF16) for j in pairs]
    n_2 = [comp[j][:, PAIR:] + n_c[2 * j + 1] for j in pairs]
    yield
    h = h_ref[q]
    outs = []
    for j in pairs:
        c0 = 2 * j
        c1 = c0 + 1
        h_b = h.astype(BF16)
        from_h = _dot(jnp.concatenate([m_2[j], m_c[c0], q_c[c0]], axis=0), h_b)
        h = from_h[:PAIR, :] + n_2[j]
        h_mid = (from_h[PAIR:2 * PAIR, :] + n_c[c0]).astype(BF16)
        o_0 = from_h[2 * PAIR:, :] + z_c[c0]
        o_1 = _dot(q_c[c1], h_mid) + z_c[c1]
        outs.append(o_0[:C, :] + o_0[C:, :])
        outs.append(o_1[:C, :] + o_1[C:, :])
        yield
    h_ref[q] = h
    o = jnp.concatenate(outs, axis=0)

    inv_n = 1.0 / RWKV_HEAD
    mu = head_sum(o, 2) * inv_n
    d = o - mu
    var = head_sum(d * d, 2) * inv_n
    y = d * lax.rsqrt(var + GN_EPS) * gn_g + gn_b
    y = y + head_sum(r * k2 * r_k, 1) * v
    o_ref[:, lanes] = (y * g).astype(o_ref.dtype)


def _rwkv_kernel(*refs, n_chunks):
    o_ref, h_ref = refs[-2:]

    @pl.when(pl.program_id(2) == 0)
    def _():
        h_ref[...] = jnp.zeros_like(h_ref)

    live = []
    for q in range(o_ref.shape[1] // PAIR):
        live.append(_rwkv_pair_stages(q, *refs, n_chunks))
        for _ in range(3):
            live = [gen for gen in live if next(gen, _DONE) is not _DONE]
    while live:
        live = [gen for gen in live if next(gen, _DONE) is not _DONE]


def _rwkv_mix(zs, lora, w_up, a_up, g_up, prm, batch, seq, d_rwkv, tt=1024, pairs_per_step=2):
    M = zs.shape[0]
    wb = pairs_per_step * PAIR
    n_cols = d_rwkv // wb
    tt = _tile(seq, tt)
    nt = seq // tt
    zspec = lambda off: pl.BlockSpec((tt, wb), lambda b, p, t: (b * nt + t, off + p))
    wspec = lambda k: pl.BlockSpec((k, wb), lambda b, p, t: (0, p))
    head_of = jnp.arange(PAIR) // RWKV_HEAD
    same_head = (head_of[:, None] == head_of[None, :]).astype(BF16)
    return pl.pallas_call(
        functools.partial(_rwkv_kernel, n_chunks=tt // CHUNK),
        out_shape=jax.ShapeDtypeStruct((M, d_rwkv), BF16),
        grid=(batch, n_cols, nt),
        in_specs=[
            zspec(0), zspec(n_cols), zspec(2 * n_cols),
            pl.BlockSpec((tt, 4 * LANES), lambda b, p, t: (b * nt + t, 0)),
            wspec(LANES), wspec(LANES), wspec(2 * LANES), wspec(8),
            pl.BlockSpec((PAIR, PAIR), lambda b, p, t: (0, 0)),
        ],
        out_specs=pl.BlockSpec((tt, wb), lambda b, p, t: (b * nt + t, p)),
        scratch_shapes=[pltpu.VMEM((pairs_per_step, PAIR, PAIR), F32)],
        compiler_params=_params("parallel", "parallel", "arbitrary"),
    )(zs, zs, zs, lora, w_up, a_up, g_up, prm, same_head)


def _conv_kernel(u_ref, w_ref, cb_ref, g_ref, b_ref, o_ref, buf_ref, acc_ref, win_ref, *, width):
    tt, ch = u_ref.shape
    t = pl.program_id(1)

    @pl.when(t == 0)
    def _():
        buf_ref[0:CONV_HALO, :] = jnp.zeros((CONV_HALO, ch), F32)

    buf_ref[CONV_HALO:, :] = u_ref[...]
    base = CONV_HALO - (width - 1)

    sub = 8

    def slab(s, _):
        cols = pl.ds(pl.multiple_of(s * LANES, LANES), LANES)
        acc = jnp.broadcast_to(cb_ref[:, cols], (tt, LANES))
        for res in range(sub):
            taps = [j for j in range(width) if (base + j) % sub == res]
            if not taps:
                continue
            span = (base + taps[-1]) // sub * sub
            win_ref[0:tt + span, :] = buf_ref[pl.ds(res, tt + span), cols]
            for j in taps:
                off = base + j - res
                acc = acc + w_ref[pl.ds(j, 1), cols] * win_ref[off:off + tt, :]
        acc_ref[:, cols] = acc
        return 0

    lax.fori_loop(0, ch // LANES, slab, 0)
    buf_ref[0:CONV_HALO, :] = buf_ref[tt:tt + CONV_HALO, :]

    g = g_ref[...]
    b = b_ref[...]
    rows = min(tt, 128)

    def norm(i, _):
        sl = pl.ds(pl.multiple_of(i * rows, rows), rows)
        y = _layer_norm_rows(acc_ref[sl, :], g, b)
        o_ref[sl, :] = (y * _sigmoid(y)).astype(o_ref.dtype)
        return 0

    lax.fori_loop(0, tt // rows, norm, 0)


def _conv_module(u, conv_w, conv_b, ln_g, ln_b, batch, seq, width, tt=256):
    M, ch = u.shape
    tt = _tile(seq, tt)
    nt = seq // tt
    vec = lambda r: pl.BlockSpec((r, ch), lambda b, t: (0, 0))
    return pl.pallas_call(
        functools.partial(_conv_kernel, width=width),
        out_shape=jax.ShapeDtypeStruct((M, ch), BF16),
        grid=(batch, nt),
        in_specs=[pl.BlockSpec((tt, ch), lambda b, t: (b * nt + t, 0)),
                  vec(conv_w.shape[0]), vec(1), vec(1), vec(1)],
        out_specs=pl.BlockSpec((tt, ch), lambda b, t: (b * nt + t, 0)),
        scratch_shapes=[pltpu.VMEM((tt + CONV_HALO, ch), F32), pltpu.VMEM((tt, ch), F32),
                        pltpu.VMEM((tt + CONV_HALO, LANES), F32)],
        compiler_params=_params("parallel", "arbitrary"),
    )(u, conv_w, conv_b, ln_g, ln_b)


def _qk_fold_kernel(wq_ref, k_ref, o_ref):
    o_ref[...] = _dot(wq_ref[...], k_ref[...], _NT).astype(o_ref.dtype)


def _qk_fold(wq, kx, batch, mem_len, heads):
    D = wq.shape[0]
    hd = D // heads
    return pl.pallas_call(
        _qk_fold_kernel,
        out_shape=jax.ShapeDtypeStruct((batch, D, heads * mem_len), BF16),
        grid=(heads, batch),
        in_specs=[pl.BlockSpec((D, hd), lambda h, b: (0, h)),
                  pl.BlockSpec((mem_len, hd), lambda h, b: (b, h))],
        out_specs=pl.BlockSpec((None, D, mem_len), lambda h, b: (b, 0, h)),
        compiler_params=_params("parallel", "arbitrary"),
    )(wq, kx)


def _vo_fold_kernel(v_ref, wo_ref, o_ref):
    o_ref[...] = _dot(v_ref[...], wo_ref[...]).astype(o_ref.dtype)


def _vo_fold(vx, wo, batch, mem_len, heads):
    D = wo.shape[0]
    hd = D // heads
    return pl.pallas_call(
        _vo_fold_kernel,
        out_shape=jax.ShapeDtypeStruct((batch, heads * mem_len, D), BF16),
        grid=(heads, batch),
        in_specs=[pl.BlockSpec((mem_len, hd), lambda h, b: (b, h)),
                  pl.BlockSpec((hd, D), lambda h, b: (h, 0))],
        out_specs=pl.BlockSpec((None, mem_len, D), lambda h, b: (b, h, 0)),
        compiler_params=_params("parallel", "arbitrary"),
    )(vx, wo)


def _attn_ln_kernel(x_ref, qk_ref, vo_ref, res_ref, g_ref, b_ref, o_ref, ob_ref, *, heads, scale, alpha):
    s = _dot(x_ref[...], qk_ref[...]) * scale
    mem_len = s.shape[1] // heads
    probs = []
    for h in range(heads):
        sh = s[:, h * mem_len:(h + 1) * mem_len]
        e = jnp.exp(sh - jnp.max(sh, axis=-1, keepdims=True))
        probs.append((e / jnp.sum(e, axis=-1, keepdims=True)).astype(BF16))
    p = jnp.concatenate(probs, axis=1)
    o_ref[...] = alpha * res_ref[...] + _dot(p, vo_ref[...])
    _ln_inplace(o_ref, ob_ref, g_ref, b_ref)


def _attn_ln(x, qk, vo, res, g, b, alpha, batch, seq, heads, tq=256):
    M, D = x.shape
    hm = qk.shape[2]
    tq = _tile(seq, tq)
    nt = seq // tq
    scale = float((D // heads) ** -0.5)
    single = pl.Buffered(1)
    tok = lambda: pl.BlockSpec((tq, D), lambda bi, t: (bi * nt + t, 0))
    vec = pl.BlockSpec((1, D), lambda bi, t: (0, 0))
    return pl.pallas_call(
        functools.partial(_attn_ln_kernel, heads=heads, scale=scale, alpha=alpha),
        out_shape=(jax.ShapeDtypeStruct((M, D), F32), jax.ShapeDtypeStruct((M, D), BF16)),
        grid=(batch, nt),
        in_specs=[tok(),
                  pl.BlockSpec((None, D, hm), lambda bi, t: (bi, 0, 0), pipeline_mode=single),
                  pl.BlockSpec((None, hm, D), lambda bi, t: (bi, 0, 0), pipeline_mode=single),
                  tok(), vec, vec],
        out_specs=(tok(), tok()),
        compiler_params=_params("parallel", "arbitrary"),
    )(x, qk, vo, res, g, b)


def _pad_cols(w, n):
    return jnp.pad(w, ((0, 0), (0, n - w.shape[1])))


def _pad_rows(w, n):
    return jnp.pad(w, ((0, n - w.shape[0]), (0, 0)))


def kernel(x, mem, w_in, rwkv_shift_mix, rwkv_w0, rwkv_w_up, rwkv_a0, rwkv_a_up, rwkv_g_up, rwkv_k_k, rwkv_k_a, rwkv_r_k, rwkv_gn_g, rwkv_gn_b, conv_w, conv_b, conv_ln_g, conv_ln_b, proj_rwkv, proj_conv, w_out, ln1_g, ln1_b, ln_mem_g, ln_mem_b, xattn_wq, xattn_wk, xattn_wv, xattn_wo, ln2_g, ln2_b, mlp_w1, mlp_w2, ln3_g, ln3_b):
    B, S, D = x.shape
    depth = w_in.shape[0]
    mem_len = mem.shape[1]
    d_rwkv = rwkv_w0.shape[1]
    d_conv = conv_b.shape[1]
    r_decay = rwkv_w_up.shape[1]
    r_iclr = rwkv_a_up.shape[1]
    r_gate = rwkv_g_up.shape[1]
    width = conv_w.shape[1]
    assert r_decay <= LANES and r_iclr <= LANES and r_gate == 2 * LANES
    assert d_rwkv % PAIR == 0 and width - 1 <= CONV_HALO
    alpha = float((2 * depth) ** 0.25)
    M = B * S
    row = lambda a: a.reshape(1, -1)

    mem_n = _ln_rows(mem.reshape(B * mem_len, D), row(ln_mem_g), row(ln_mem_b), BF16)
    h = x.reshape(M, D)
    hb = h.astype(BF16)
    for l in range(depth):
        w = w_in[l]
        c_lora = 3 * d_rwkv
        c_conv = c_lora + r_decay + r_iclr + r_gate
        c_gate = c_conv + 2 * d_conv
        pad = -c_conv % W_COL_TILE
        assert c_lora % W_COL_TILE == 0
        w_all = jnp.concatenate([w[:, :c_conv], jnp.zeros((D, pad), F32), w[:, c_conv:]], axis=1).astype(BF16)
        c_conv_p = c_conv + pad
        c_gate_p = c_gate + pad
        mix = rwkv_shift_mix[l]
        zs = _mm_shift(hb, w_all, row(mix[:c_lora]), S)
        w_lora = jnp.concatenate([
            _pad_cols(w_all[:, c_lora:c_lora + r_decay], LANES),
            _pad_cols(w_all[:, c_lora + r_decay:c_lora + r_decay + r_iclr], LANES),
            w_all[:, c_lora + r_decay + r_iclr:c_conv]], axis=1)
        mu_lora = jnp.concatenate([
            jnp.pad(mix[c_lora:c_lora + r_decay], (0, LANES - r_decay)),
            jnp.pad(mix[c_lora + r_decay:c_lora + r_decay + r_iclr], (0, LANES - r_iclr)),
            mix[c_lora + r_decay + r_iclr:]]).reshape(1, -1)
        lora = _lora_act(_mm_shift(hb, w_lora, mu_lora, S), 0)
        prm = jnp.stack([rwkv_w0[l], rwkv_a0[l], rwkv_k_k[l], rwkv_k_a[l], rwkv_r_k[l].reshape(-1),
                         rwkv_gn_g[l], rwkv_gn_b[l], jnp.zeros((d_rwkv,), F32)])
        o_r = _rwkv_mix(zs, lora, _pad_rows(rwkv_w_up[l], LANES), _pad_rows(rwkv_a_up[l], LANES),
                        rwkv_g_up[l], prm, B, S, d_rwkv)

        u = _mm_glu(hb, w_all, c_conv_p, c_conv_p + d_conv, d_conv)
        cw = _pad_rows(conv_w[l].reshape(width, d_conv), CONV_HALO)
        o_c = _conv_module(u, cw, row(conv_b[l]), row(conv_ln_g[l]), row(conv_ln_b[l]), B, S, width)

        merged = _mm_merge(hb, o_r, o_c, w_all, c_gate_p, c_gate_p + D,
                           proj_rwkv[l].astype(BF16), proj_conv[l].astype(BF16))
        h, hb = _mm_res_ln(merged, w_out[l], h, row(ln1_g[l]), row(ln1_b[l]), alpha)

        kx = _mm_plain(mem_n, xattn_wk[l].astype(BF16), BF16)
        vx = _mm_plain(mem_n, xattn_wv[l].astype(BF16), BF16)
        qk = _qk_fold(xattn_wq[l].astype(BF16), kx, B, mem_len, XATTN_HEADS)
        vo = _vo_fold(vx, xattn_wo[l].astype(BF16), B, mem_len, XATTN_HEADS)
        h, hb = _attn_ln(hb, qk, vo, h, row(ln2_g[l]), row(ln2_b[l]), alpha, B, S, XATTN_HEADS)

        h = _mlp_ln(hb, mlp_w1[l], mlp_w2[l], h, row(ln3_g[l]), row(ln3_b[l]), alpha)
        hb = h.astype(BF16)
    return h.reshape(B, S, D)
```

```python
import functools
import math

import jax
import jax.numpy as jnp
from jax import lax
from jax.experimental import pallas as pl
from jax.experimental.pallas import tpu as pltpu

F32 = jnp.float32
BF16 = jnp.bfloat16

LANES = 128
RWKV_HEAD = 64
PAIR = 2 * RWKV_HEAD
CHUNK = 64
CONV_HALO = 32
XATTN_HEADS = 4
LN_EPS = 1e-5
GN_EPS = 64e-5
VMEM_LIMIT = 56 * 1024 * 1024

_NN = (((1,), (0,)), ((), ()))
_NT = (((1,), (1,)), ((), ()))
_TN = (((0,), (0,)), ((), ()))
_DONE = object()


def _params(*sem):
    return pltpu.CompilerParams(dimension_semantics=sem, vmem_limit_bytes=VMEM_LIMIT)


def _dot(a, b, dims=_NN):
    return lax.dot_general(a, b, dims, preferred_element_type=F32)


def _sigmoid(x):
    return 1.0 / (1.0 + jnp.exp(-x))


def _layer_norm_rows(v, g, b):
    mu = jnp.mean(v, axis=-1, keepdims=True)
    d = v - mu
    var = jnp.mean(d * d, axis=-1, keepdims=True)
    return d * lax.rsqrt(var + LN_EPS) * g + b


def _tile(n, pref):
    t = min(n, pref)
    assert n % t == 0, (n, pref)
    return t


def _mm_shift_kernel(x_ref, w_ref, mu_ref, o_ref, carry_ref, *, tiles_per_seq):
    m = pl.program_id(1)
    z = _dot(x_ref[...], w_ref[...])
    tm = z.shape[0]

    @pl.when(m % tiles_per_seq == 0)
    def _():
        carry_ref[...] = jnp.zeros_like(carry_ref)

    prev_row = carry_ref[0:1, :]
    row = lax.broadcasted_iota(jnp.int32, z.shape, 0)
    prev = jnp.where(row == 0, prev_row, pltpu.roll(z, 1, axis=0))
    o_ref[...] = z + (prev - z) * mu_ref[...]
    carry_ref[0:1, :] = z[tm - 1:tm, :]


def _mm_shift(x, w, mu, seq, tm=2048, tn=512):
    M, K = x.shape
    N = w.shape[1]
    tm = _tile(seq, tm)
    tn = _tile(N, tn)
    return pl.pallas_call(
        functools.partial(_mm_shift_kernel, tiles_per_seq=seq // tm),
        out_shape=jax.ShapeDtypeStruct((M, N), F32),
        grid=(N // tn, M // tm),
        in_specs=[
            pl.BlockSpec((tm, K), lambda n, m: (m, 0)),
            pl.BlockSpec((K, tn), lambda n, m: (0, n)),
            pl.BlockSpec((1, tn), lambda n, m: (0, n)),
        ],
        out_specs=pl.BlockSpec((tm, tn), lambda n, m: (m, n)),
        scratch_shapes=[pltpu.VMEM((8, tn), F32)],
        compiler_params=_params("parallel", "arbitrary"),
    )(x, w, mu)


def _mm_glu_kernel(x_ref, wa_ref, wb_ref, o_ref):
    x = x_ref[...]
    o_ref[...] = _dot(x, wa_ref[...]) * _sigmoid(_dot(x, wb_ref[...]))


def _mm_glu(x, wa, wb, tm=1024, tn=512):
    M, K = x.shape
    N = wa.shape[1]
    tm = _tile(M, tm)
    tn = _tile(N, tn)
    return pl.pallas_call(
        _mm_glu_kernel,
        out_shape=jax.ShapeDtypeStruct((M, N), F32),
        grid=(N // tn, M // tm),
        in_specs=[
            pl.BlockSpec((tm, K), lambda n, m: (m, 0)),
            pl.BlockSpec((K, tn), lambda n, m: (0, n)),
            pl.BlockSpec((K, tn), lambda n, m: (0, n)),
        ],
        out_specs=pl.BlockSpec((tm, tn), lambda n, m: (m, n)),
        compiler_params=_params("parallel", "arbitrary"),
    )(x, wa, wb)


def _mm_plain_kernel(x_ref, w_ref, o_ref):
    o_ref[...] = _dot(x_ref[...], w_ref[...]).astype(o_ref.dtype)


def _mm_plain(x, w, out_dtype, tm=1024, tn=512):
    M, K = x.shape
    N = w.shape[1]
    tm = _tile(M, tm)
    tn = _tile(N, tn)
    return pl.pallas_call(
        _mm_plain_kernel,
        out_shape=jax.ShapeDtypeStruct((M, N), out_dtype),
        grid=(N // tn, M // tm),
        in_specs=[
            pl.BlockSpec((tm, K), lambda n, m: (m, 0)),
            pl.BlockSpec((K, tn), lambda n, m: (0, n)),
        ],
        out_specs=pl.BlockSpec((tm, tn), lambda n, m: (m, n)),
        compiler_params=_params("parallel", "arbitrary"),
    )(x, w)


def _mm_merge_kernel(x_ref, orw_ref, ocv_ref, wgr_ref, wgc_ref, pr_ref, pc_ref, o_ref):
    x = x_ref[...]
    gate_r = _sigmoid(_dot(x, wgr_ref[...]))
    gate_c = _sigmoid(_dot(x, wgc_ref[...]))
    merged = gate_r * _dot(orw_ref[...], pr_ref[...]) + gate_c * _dot(ocv_ref[...], pc_ref[...])
    o_ref[...] = merged.astype(o_ref.dtype)


def _mm_merge(x, o_r, o_c, wgr, wgc, p_r, p_c, tm=512, tn=512):
    M, K = x.shape
    Kb = o_r.shape[1]
    N = wgr.shape[1]
    tm = _tile(M, tm)
    tn = _tile(N, tn)
    row = lambda k: pl.BlockSpec((tm, k), lambda n, m: (m, 0))
    col = lambda k: pl.BlockSpec((k, tn), lambda n, m: (0, n))
    return pl.pallas_call(
        _mm_merge_kernel,
        out_shape=jax.ShapeDtypeStruct((M, N), BF16),
        grid=(N // tn, M // tm),
        in_specs=[row(K), row(Kb), row(Kb), col(K), col(K), col(Kb), col(Kb)],
        out_specs=pl.BlockSpec((tm, tn), lambda n, m: (m, n)),
        compiler_params=_params("parallel", "arbitrary"),
    )(x, o_r, o_c, wgr, wgc, p_r, p_c)


def _ln_inplace(o_ref, ob_ref, g_ref, b_ref, rows=128):
    g = g_ref[...]
    b = b_ref[...]

    def body(i, _):
        sl = pl.ds(pl.multiple_of(i * rows, rows), rows)
        y = _layer_norm_rows(o_ref[sl, :], g, b)
        o_ref[sl, :] = y
        if ob_ref is not None:
            ob_ref[sl, :] = y.astype(ob_ref.dtype)
        return 0

    lax.fori_loop(0, o_ref.shape[0] // rows, body, 0)


def _mm_res_ln_kernel(y_ref, w_ref, res_ref, g_ref, b_ref, o_ref, ob_ref, *, alpha, tn):
    n = pl.program_id(1)
    col = pl.ds(pl.multiple_of(n * tn, tn), tn)
    o_ref[:, col] = alpha * res_ref[...] + _dot(y_ref[...], w_ref[...])

    @pl.when(n == pl.num_programs(1) - 1)
    def _():
        _ln_inplace(o_ref, ob_ref, g_ref, b_ref)


def _mm_res_ln(y, w, res, g, b, alpha, tm=512, tn=512):
    M, K = y.shape
    N = w.shape[1]
    tm = _tile(M, tm)
    tn = _tile(N, tn)
    return pl.pallas_call(
        functools.partial(_mm_res_ln_kernel, alpha=alpha, tn=tn),
        out_shape=(jax.ShapeDtypeStruct((M, N), F32), jax.ShapeDtypeStruct((M, N), BF16)),
        grid=(M // tm, N // tn),
        in_specs=[
            pl.BlockSpec((tm, K), lambda m, n: (m, 0)),
            pl.BlockSpec((K, tn), lambda m, n: (0, n)),
            pl.BlockSpec((tm, tn), lambda m, n: (m, n)),
            pl.BlockSpec((1, N), lambda m, n: (0, 0)),
            pl.BlockSpec((1, N), lambda m, n: (0, 0)),
        ],
        out_specs=(pl.BlockSpec((tm, N), lambda m, n: (m, 0)),
                   pl.BlockSpec((tm, N), lambda m, n: (m, 0))),
        compiler_params=_params("parallel", "arbitrary"),
    )(y, w.astype(BF16), res, g, b)


def _mlp_ln_kernel(x_ref, w1_ref, w2_ref, res_ref, g_ref, b_ref, o_ref, *, alpha, tn):
    k = pl.program_id(1)
    h = jnp.maximum(_dot(x_ref[...], w1_ref[...]), 0.0)
    h = (h * h).astype(BF16)
    n_slabs = o_ref.shape[1] // tn

    @pl.when(k == 0)
    def _():
        for j in range(n_slabs):
            sl = slice(j * tn, (j + 1) * tn)
            o_ref[:, sl] = _dot(h, w2_ref[:, sl])

    @pl.when(k > 0)
    def _():
        for j in range(n_slabs):
            sl = slice(j * tn, (j + 1) * tn)
            o_ref[:, sl] += _dot(h, w2_ref[:, sl])

    @pl.when(k < n_slabs)
    def _():
        col = pl.ds(pl.multiple_of(k * tn, tn), tn)
        o_ref[:, col] += alpha * res_ref[...]

    @pl.when(k == pl.num_programs(1) - 1)
    def _():
        _ln_inplace(o_ref, None, g_ref, b_ref)


def _mlp_ln(x, w1, w2, res, g, b, alpha, tm=1024, tk=512, tn=512):
    M, D = x.shape
    FF = w1.shape[1]
    tm = _tile(M, tm)
    tk = _tile(FF, tk)
    tn = _tile(D, tn)
    n_slabs = D // tn
    assert FF // tk >= n_slabs
    single = pl.Buffered(1)
    return pl.pallas_call(
        functools.partial(_mlp_ln_kernel, alpha=alpha, tn=tn),
        out_shape=jax.ShapeDtypeStruct((M, D), F32),
        grid=(M // tm, FF // tk),
        in_specs=[
            pl.BlockSpec((tm, D), lambda m, k: (m, 0), pipeline_mode=single),
            pl.BlockSpec((D, tk), lambda m, k: (0, k)),
            pl.BlockSpec((tk, D), lambda m, k: (k, 0)),
            pl.BlockSpec((tm, tn), lambda m, k: (m, jnp.minimum(k, n_slabs - 1))),
            pl.BlockSpec((1, D), lambda m, k: (0, 0)),
            pl.BlockSpec((1, D), lambda m, k: (0, 0)),
        ],
        out_specs=pl.BlockSpec((tm, D), lambda m, k: (m, 0), pipeline_mode=single),
        compiler_params=_params("parallel", "arbitrary"),
    )(x, w1.astype(BF16), w2.astype(BF16), res, g, b)


def _ln_rows_kernel(x_ref, g_ref, b_ref, o_ref):
    o_ref[...] = _layer_norm_rows(x_ref[...], g_ref[...], b_ref[...]).astype(o_ref.dtype)


def _ln_rows(x, g, b, out_dtype, tm=128):
    M, D = x.shape
    tm = _tile(M, tm)
    return pl.pallas_call(
        _ln_rows_kernel,
        out_shape=jax.ShapeDtypeStruct((M, D), out_dtype),
        grid=(M // tm,),
        in_specs=[pl.BlockSpec((tm, D), lambda m: (m, 0)),
                  pl.BlockSpec((1, D), lambda m: (0, 0)),
                  pl.BlockSpec((1, D), lambda m: (0, 0))],
        out_specs=pl.BlockSpec((tm, D), lambda m: (m, 0)),
        compiler_params=_params("parallel"),
    )(x, g, b)


def _lora_act_kernel(z_ref, o_ref):
    o_ref[:, 0:LANES] = jnp.tanh(z_ref[:, 0:LANES])
    o_ref[:, LANES:2 * LANES] = z_ref[:, LANES:2 * LANES]
    o_ref[:, 2 * LANES:] = _sigmoid(z_ref[:, 2 * LANES:])


def _lora_act(zs, col_block, tm=1024):
    M = zs.shape[0]
    W = 4 * LANES
    tm = _tile(M, tm)
    return pl.pallas_call(
        _lora_act_kernel,
        out_shape=jax.ShapeDtypeStruct((M, W), F32),
        grid=(M // tm,),
        in_specs=[pl.BlockSpec((tm, W), lambda m: (m, col_block))],
        out_specs=pl.BlockSpec((tm, W), lambda m: (m, 0)),
        compiler_params=_params("parallel"),
    )(zs)


def _rwkv_pair_stages(q, zr_ref, zk_ref, zv_ref, l_ref, wup_ref, aup_ref, gup_ref, prm_ref, ones_ref,
                      o_ref, h_ref, n_chunks):
    C = CHUNK
    lane = lax.broadcasted_iota(jnp.int32, (1, PAIR), 1)
    m0 = (lane < RWKV_HEAD).astype(F32)
    m1 = 1.0 - m0
    ri = lax.broadcasted_iota(jnp.int32, (PAIR, PAIR), 0)
    ci = lax.broadcasted_iota(jnp.int32, (PAIR, PAIR), 1)
    strict = ci < ri
    lower = ci <= ri
    eye = (ri == ci).astype(F32)
    head_ones = ones_ref[...]
    rc = lax.broadcasted_iota(jnp.int32, (C, C), 0)
    cc = lax.broadcasted_iota(jnp.int32, (C, C), 1)
    tril_c = (cc <= rc).astype(BF16)
    assert n_chunks % 2 == 0

    lanes = slice(q * PAIR, (q + 1) * PAIR)
    prm = prm_ref[:, lanes]
    w0, a0, k_k, k_a, r_k, gn_g, gn_b = (prm[i:i + 1, :] for i in range(7))
    w_up = wup_ref[:, lanes].astype(BF16)
    a_up = aup_ref[:, lanes].astype(BF16)
    g_up = gup_ref[:, lanes].astype(BF16)

    def stack(x):
        return jnp.concatenate([x * m0, x * m1], axis=0)

    def head_sum(x, passes):
        acc = None
        for _ in range(passes):
            piece = x.astype(BF16)
            part = _dot(piece, head_ones)
            acc = part if acc is None else acc + part
            x = x - piece.astype(F32)
        return acc

    def bdot(a, b, dims=_NN):
        return _dot(a.astype(BF16), b.astype(BF16), dims)

    r = zr_ref[:, lanes]
    k = zk_ref[:, lanes]
    v = zv_ref[:, lanes]
    lora = l_ref[...].astype(BF16)
    w_raw = w0 + _dot(lora[:, 0:LANES], w_up)
    a_raw = a0 + _dot(lora[:, LANES:2 * LANES], a_up)
    g = _dot(lora[:, 2 * LANES:], g_up)
    wl = -math.exp(-0.5) * _sigmoid(w_raw)
    a_sig = _sigmoid(a_raw)
    kk = k * k_k
    kk = kk * lax.rsqrt(jnp.maximum(head_sum(kk * kk, 1), 1e-24))
    k2 = k * (1.0 + (a_sig - 1.0) * k_a)
    b_s = kk * a_sig
    yield

    chunks = range(n_chunks)
    rows = [slice(c * C, (c + 1) * C) for c in chunks]
    wl_hi = wl.astype(BF16)
    wl_r = wl - wl_hi.astype(F32)
    wl_mid = wl_r.astype(BF16)
    wl_lo = (wl_r - wl_mid.astype(F32)).astype(BF16)
    wl_parts = jnp.concatenate([wl_hi, wl_mid, wl_lo], axis=1)
    cum_parts = [_dot(tril_c, wl_parts[s]) for s in rows]
    cum_c = [m[:, :PAIR] + (m[:, PAIR:2 * PAIR] + m[:, 2 * PAIR:]) for m in cum_parts]
    cum = jnp.concatenate(cum_c, axis=0)
    cum_end = jnp.concatenate([jnp.broadcast_to(m[C - 1:C, :], (C, PAIR)) for m in cum_c], axis=0)
    e_neg = jnp.exp(-cum)
    e_tail = jnp.exp(cum_end - cum)
    r_t = r * jnp.exp(cum)
    a_t = -kk * jnp.exp(cum - wl)
    b_t = b_s * e_neg
    k_t = k2 * e_neg
    bw = b_s * e_tail
    kw = k2 * e_tail
    w_end = jnp.exp(cum_end)
    yield

    a_st = [stack(a_t[s]).astype(BF16) for s in rows]
    r_st = [stack(r_t[s]) for s in rows]
    v_st = [stack(v[s]).astype(BF16) for s in rows]
    gram = [bdot(jnp.concatenate([a_st[c], r_st[c].astype(BF16)], axis=0),
                 jnp.concatenate([stack(b_t[rows[c]]), stack(k_t[rows[c]])], axis=0), _NT)
            for c in chunks]
    yield
    a_ab =[jnp.where(strict, m[:PAIR, :PAIR], 0.0) for m in gram]
    a_ak = [jnp.where(strict, m[:PAIR, PAIR:], 0.0).astype(BF16) for m in gram]
    g_rb = [jnp.where(lower, m[PAIR:, :PAIR], 0.0).astype(BF16) for m in gram]
    g_rk = [jnp.where(lower, m[PAIR:, PAIR:], 0.0).astype(BF16) for m in gram]

    pw = [m.astype(BF16) for m in a_ab]
    pw = [_dot(m, m).astype(BF16) for m in pw]
    t_inv = [eye + m for m in a_ab]
    yield
    n = 2
    while 2 * n < C:
        both = [_dot(pw[c], jnp.concatenate([pw[c], t_inv[c].astype(BF16)], axis=1)) for c in chunks]
        pw = [m[:, :PAIR].astype(BF16) for m in both]
        t_inv = [t_inv[c] + both[c][:, PAIR:] for c in chunks]
        n *= 2
        yield
    t_b = [(t_inv[c] + _dot(pw[c], t_inv[c].astype(BF16))).astype(BF16) for c in chunks]

    a_v = [_dot(a_ak[c], v_st[c]).astype(BF16) for c in chunks]
    yield
    t_av =[_dot(t_b[c], jnp.concatenate([a_st[c], a_v[c]], axis=1)).astype(BF16)
            for c in chunks]
    zero = jnp.zeros((PAIR, PAIR), BF16)
    rhs2 = [jnp.concatenate([t_av[c], jnp.concatenate([zero, v_st[c]], axis=1)], axis=0)
            for c in chunks]
    yield
    bw_st = [stack(bw[s]).astype(BF16) for s in rows]
    kw_st = [stack(kw[s]).astype(BF16) for s in rows]
    mn = [_dot(jnp.concatenate([bw_st[c], kw_st[c]], axis=0), rhs2[c], _TN) for c in chunks]
    qz = [_dot(jnp.concatenate([g_rb[c], g_rk[c]], axis=1), rhs2[c]) for c in chunks]
    m_c = [(eye * w_end[rows[c]][0:1, :] + mn[c][:, :PAIR]).astype(BF16) for c in chunks]
    n_c = [mn[c][:, PAIR:] for c in chunks]
    q_c = [(r_st[c] + qz[c][:, :PAIR]).astype(BF16) for c in chunks]
    z_c = [qz[c][:, PAIR:] for c in chunks]
    yield

    pairs = range(n_chunks // 2)
    comp = [_dot(m_c[2 * j + 1], jnp.concatenate([m_c[2 * j], n_c[2 * j].astype(BF16)], axis=1))
            for j in pairs]
    m_2 = [comp[j][:, :PAIR].astype(BF16) for j in pairs]
    n_2 = [comp[j][:, PAIR:] + n_c[2 * j + 1] for j in pairs]
    yield
    h = h_ref[q]
    outs = []
    for j in pairs:
        c0 = 2 * j
        c1 = c0 + 1
        h_b = h.astype(BF16)
        from_h = _dot(jnp.concatenate([m_2[j], m_c[c0], q_c[c0]], axis=0), h_b)
        h = from_h[:PAIR, :] + n_2[j]
        h_mid = (from_h[PAIR:2 * PAIR, :] + n_c[c0]).astype(BF16)
        o_0 = from_h[2 * PAIR:, :] + z_c[c0]
        o_1 = _dot(q_c[c1], h_mid) + z_c[c1]
        outs.append(o_0[:C, :] + o_0[C:, :])
        outs.append(o_1[:C, :] + o_1[C:, :])
        yield
    h_ref[q] = h
    o = jnp.concatenate(outs, axis=0)

    inv_n = 1.0 / RWKV_HEAD
    mu = head_sum(o, 2) * inv_n
    d = o - mu
    var = head_sum(d * d, 2) * inv_n
    y = d * lax.rsqrt(var + GN_EPS) * gn_g + gn_b
    y = y + head_sum(r * k2 * r_k, 1) * v
    o_ref[:, lanes] = (y * g).astype(o_ref.dtype)


def _rwkv_kernel(*refs, n_chunks):
    o_ref, h_ref = refs[-2:]

    @pl.when(pl.program_id(2) == 0)
    def _():
        h_ref[...] = jnp.zeros_like(h_ref)

    live = []
    for q in range(o_ref.shape[1] // PAIR):
        live.append(_rwkv_pair_stages(q, *refs, n_chunks))
        for _ in range(3):
            live = [gen for gen in live if next(gen, _DONE) is not _DONE]
    while live:
        live = [gen for gen in live if next(gen, _DONE) is not _DONE]


def _rwkv_mix(zs, lora, w_up, a_up, g_up, prm, batch, seq, d_rwkv, tt=1024, pairs_per_step=2):
    M = zs.shape[0]
    wb = pairs_per_step * PAIR
    n_cols = d_rwkv // wb
    tt = _tile(seq, tt)
    nt = seq // tt
    zspec = lambda off: pl.BlockSpec((tt, wb), lambda b, p, t: (b * nt + t, off + p))
    wspec = lambda k: pl.BlockSpec((k, wb), lambda b, p, t: (0, p))
    head_of = jnp.arange(PAIR) // RWKV_HEAD
    same_head = (head_of[:, None] == head_of[None, :]).astype(BF16)
    return pl.pallas_call(
        functools.partial(_rwkv_kernel, n_chunks=tt // CHUNK),
        out_shape=jax.ShapeDtypeStruct((M, d_rwkv), BF16),
        grid=(batch, n_cols, nt),
        in_specs=[
            zspec(0), zspec(n_cols), zspec(2 * n_cols),
            pl.BlockSpec((tt, 4 * LANES), lambda b, p, t: (b * nt + t, 0)),
            wspec(LANES), wspec(LANES), wspec(2 * LANES), wspec(8),
            pl.BlockSpec((PAIR, PAIR), lambda b, p, t: (0, 0)),
        ],
        out_specs=pl.BlockSpec((tt, wb), lambda b, p, t: (b * nt + t, p)),
        scratch_shapes=[pltpu.VMEM((pairs_per_step, PAIR, PAIR), F32)],
        compiler_params=_params("parallel", "parallel", "arbitrary"),
    )(zs, zs, zs, lora, w_up, a_up, g_up, prm, same_head)


def _conv_kernel(u_ref, w_ref, cb_ref, g_ref, b_ref, o_ref, buf_ref, acc_ref, win_ref, *, width):
    tt, ch = u_ref.shape
    t = pl.program_id(1)

    @pl.when(t == 0)
    def _():
        buf_ref[0:CONV_HALO, :] = jnp.zeros((CONV_HALO, ch), F32)

    buf_ref[CONV_HALO:, :] = u_ref[...]
    base = CONV_HALO - (width - 1)

    sub = 8

    def slab(s, _):
        cols = pl.ds(pl.multiple_of(s * LANES, LANES), LANES)
        acc = jnp.broadcast_to(cb_ref[:, cols], (tt, LANES))
        for res in range(sub):
            taps = [j for j in range(width) if (base + j) % sub == res]
            if not taps:
                continue
            span = (base + taps[-1]) // sub * sub
            win_ref[0:tt + span, :] = buf_ref[pl.ds(res, tt + span), cols]
            for j in taps:
                off = base + j - res
                acc = acc + w_ref[pl.ds(j, 1), cols] * win_ref[off:off + tt, :]
        acc_ref[:, cols] = acc
        return 0

    lax.fori_loop(0, ch // LANES, slab, 0)
    buf_ref[0:CONV_HALO, :] = buf_ref[tt:tt + CONV_HALO, :]

    g = g_ref[...]
    b = b_ref[...]
    rows = min(tt, 128)

    def norm(i, _):
        sl = pl.ds(pl.multiple_of(i * rows, rows), rows)
        y = _layer_norm_rows(acc_ref[sl, :], g, b)
        o_ref[sl, :] = (y * _sigmoid(y)).astype(o_ref.dtype)
        return 0

    lax.fori_loop(0, tt // rows, norm, 0)


def _conv_module(u, conv_w, conv_b, ln_g, ln_b, batch, seq, width, tt=256):
    M, ch = u.shape
    tt = _tile(seq, tt)
    nt = seq // tt
    vec = lambda r: pl.BlockSpec((r, ch), lambda b, t: (0, 0))
    return pl.pallas_call(
        functools.partial(_conv_kernel, width=width),
        out_shape=jax.ShapeDtypeStruct((M, ch), BF16),
        grid=(batch, nt),
        in_specs=[pl.BlockSpec((tt, ch), lambda b, t: (b * nt + t, 0)),
                  vec(conv_w.shape[0]), vec(1), vec(1), vec(1)],
        out_specs=pl.BlockSpec((tt, ch), lambda b, t: (b * nt + t, 0)),
        scratch_shapes=[pltpu.VMEM((tt + CONV_HALO, ch), F32), pltpu.VMEM((tt, ch), F32),
                        pltpu.VMEM((tt + CONV_HALO, LANES), F32)],
        compiler_params=_params("parallel", "arbitrary"),
    )(u, conv_w, conv_b, ln_g, ln_b)


def _qk_fold_kernel(wq_ref, k_ref, o_ref):
    o_ref[...] = _dot(wq_ref[...], k_ref[...], _NT).astype(o_ref.dtype)


def _qk_fold(wq, kx, batch, mem_len, heads):
    D = wq.shape[0]
    hd = D // heads
    return pl.pallas_call(
        _qk_fold_kernel,
        out_shape=jax.ShapeDtypeStruct((batch, D, heads * mem_len), BF16),
        grid=(heads, batch),
        in_specs=[pl.BlockSpec((D, hd), lambda h, b: (0, h)),
                  pl.BlockSpec((mem_len, hd), lambda h, b: (b, h))],
        out_specs=pl.BlockSpec((None, D, mem_len), lambda h, b: (b, 0, h)),
        compiler_params=_params("parallel", "arbitrary"),
    )(wq, kx)


def _vo_fold_kernel(v_ref, wo_ref, o_ref):
    o_ref[...] = _dot(v_ref[...], wo_ref[...]).astype(o_ref.dtype)


def _vo_fold(vx, wo, batch, mem_len, heads):
    D = wo.shape[0]
    hd = D // heads
    return pl.pallas_call(
        _vo_fold_kernel,
        out_shape=jax.ShapeDtypeStruct((batch, heads * mem_len, D), BF16),
        grid=(heads, batch),
        in_specs=[pl.BlockSpec((mem_len, hd), lambda h, b: (b, h)),
                  pl.BlockSpec((hd, D), lambda h, b: (h, 0))],
        out_specs=pl.BlockSpec((None, mem_len, D), lambda h, b: (b, h, 0)),
        compiler_params=_params("parallel", "arbitrary"),
    )(vx, wo)


def _attn_ln_kernel(x_ref, qk_ref, vo_ref, res_ref, g_ref, b_ref, o_ref, ob_ref, *, heads, scale, alpha):
    s = _dot(x_ref[...], qk_ref[...]) * scale
    mem_len = s.shape[1] // heads
    probs = []
    for h in range(heads):
        sh = s[:, h * mem_len:(h + 1) * mem_len]
        e = jnp.exp(sh - jnp.max(sh, axis=-1, keepdims=True))
        probs.append((e / jnp.sum(e, axis=-1, keepdims=True)).astype(BF16))
    p = jnp.concatenate(probs, axis=1)
    o_ref[...] = alpha * res_ref[...] + _dot(p, vo_ref[...])
    _ln_inplace(o_ref, ob_ref, g_ref, b_ref)


def _attn_ln(x, qk, vo, res, g, b, alpha, batch, seq, heads, tq=256):
    M, D = x.shape
    hm = qk.shape[2]
    tq = _tile(seq, tq)
    nt = seq // tq
    scale = float((D // heads) ** -0.5)
    single = pl.Buffered(1)
    tok = lambda: pl.BlockSpec((tq, D), lambda bi, t: (bi * nt + t, 0))
    vec = pl.BlockSpec((1, D), lambda bi, t: (0, 0))
    return pl.pallas_call(
        functools.partial(_attn_ln_kernel, heads=heads, scale=scale, alpha=alpha),
        out_shape=(jax.ShapeDtypeStruct((M, D), F32), jax.ShapeDtypeStruct((M, D), BF16)),
        grid=(batch, nt),
        in_specs=[tok(),
                  pl.BlockSpec((None, D, hm), lambda bi, t: (bi, 0, 0), pipeline_mode=single),
                  pl.BlockSpec((None, hm, D), lambda bi, t: (bi, 0, 0), pipeline_mode=single),
                  tok(), vec, vec],
        out_specs=(tok(), tok()),
        compiler_params=_params("parallel", "arbitrary"),
    )(x, qk, vo, res, g, b)


def _pad_cols(w, n):
    return jnp.pad(w, ((0, 0), (0, n - w.shape[1])))


def _pad_rows(w, n):
    return jnp.pad(w, ((0, n - w.shape[0]), (0, 0)))


def kernel(x, mem, w_in, rwkv_shift_mix, rwkv_w0, rwkv_w_up, rwkv_a0, rwkv_a_up, rwkv_g_up, rwkv_k_k, rwkv_k_a, rwkv_r_k, rwkv_gn_g, rwkv_gn_b, conv_w, conv_b, conv_ln_g, conv_ln_b, proj_rwkv, proj_conv, w_out, ln1_g, ln1_b, ln_mem_g, ln_mem_b, xattn_wq, xattn_wk, xattn_wv, xattn_wo, ln2_g, ln2_b, mlp_w1, mlp_w2, ln3_g, ln3_b):
    B, S, D = x.shape
    depth = w_in.shape[0]
    mem_len = mem.shape[1]
    d_rwkv = rwkv_w0.shape[1]
    d_conv = conv_b.shape[1]
    r_decay = rwkv_w_up.shape[1]
    r_iclr = rwkv_a_up.shape[1]
    r_gate = rwkv_g_up.shape[1]
    width = conv_w.shape[1]
    assert r_decay <= LANES and r_iclr <= LANES and r_gate == 2 * LANES
    assert d_rwkv % PAIR == 0 and width - 1 <= CONV_HALO
    alpha = float((2 * depth) ** 0.25)
    M = B * S
    row = lambda a: a.reshape(1, -1)

    mem_n = _ln_rows(mem.reshape(B * mem_len, D), row(ln_mem_g), row(ln_mem_b), BF16)
    h = x.reshape(M, D)
    hb = h.astype(BF16)
    for l in range(depth):
        w = w_in[l]
        c_lora = 3 * d_rwkv
        c_conv = c_lora + r_decay + r_iclr + r_gate
        c_gate = c_conv + 2 * d_conv
        mix = rwkv_shift_mix[l]
        zs = _mm_shift(hb, w[:, :c_lora].astype(BF16), row(mix[:c_lora]), S)
        w_lora = jnp.concatenate([
            _pad_cols(w[:, c_lora:c_lora + r_decay], LANES),
            _pad_cols(w[:, c_lora + r_decay:c_lora + r_decay + r_iclr], LANES),
            w[:, c_lora + r_decay + r_iclr:c_conv]], axis=1).astype(BF16)
        mu_lora = jnp.concatenate([
            jnp.pad(mix[c_lora:c_lora + r_decay], (0, LANES - r_decay)),
            jnp.pad(mix[c_lora + r_decay:c_lora + r_decay + r_iclr], (0, LANES - r_iclr)),
            mix[c_lora + r_decay + r_iclr:]]).reshape(1, -1)
        lora = _lora_act(_mm_shift(hb, w_lora, mu_lora, S), 0)
        prm = jnp.stack([rwkv_w0[l], rwkv_a0[l], rwkv_k_k[l], rwkv_k_a[l], rwkv_r_k[l].reshape(-1),
                         rwkv_gn_g[l], rwkv_gn_b[l], jnp.zeros((d_rwkv,), F32)])
        o_r = _rwkv_mix(zs, lora, _pad_rows(rwkv_w_up[l], LANES), _pad_rows(rwkv_a_up[l], LANES),
                        rwkv_g_up[l], prm, B, S, d_rwkv)

        u = _mm_glu(hb, w[:, c_conv:c_conv + d_conv].astype(BF16),
                    w[:, c_conv + d_conv:c_gate].astype(BF16))
        cw = _pad_rows(conv_w[l].reshape(width, d_conv), CONV_HALO)
        o_c = _conv_module(u, cw, row(conv_b[l]), row(conv_ln_g[l]), row(conv_ln_b[l]), B, S, width)

        merged = _mm_merge(hb, o_r, o_c, w[:, c_gate:c_gate + D].astype(BF16),
                           w[:, c_gate + D:].astype(BF16),
                           proj_rwkv[l].astype(BF16), proj_conv[l].astype(BF16))
        h, hb = _mm_res_ln(merged, w_out[l], h, row(ln1_g[l]), row(ln1_b[l]), alpha)

        kx = _mm_plain(mem_n, xattn_wk[l].astype(BF16), BF16)
        vx = _mm_plain(mem_n, xattn_wv[l].astype(BF16), BF16)
        qk = _qk_fold(xattn_wq[l].astype(BF16), kx, B, mem_len, XATTN_HEADS)
        vo = _vo_fold(vx, xattn_wo[l].astype(BF16), B, mem_len, XATTN_HEADS)
        h, hb = _attn_ln(hb, qk, vo, h, row(ln2_g[l]), row(ln2_b[l]), alpha, B, S, XATTN_HEADS)

        h = _mlp_ln(hb, mlp_w1[l], mlp_w2[l], h, row(ln3_g[l]), row(ln3_b[l]), alpha)
        hb = h.astype(BF16)
    return h.reshape(B, S, D)
```

```python
import functools
import math

import jax
import jax.numpy as jnp
from jax import lax
from jax.experimental import pallas as pl
from jax.experimental.pallas import tpu as pltpu

F32 = jnp.float32
BF16 = jnp.bfloat16

LANES = 128
RWKV_HEAD = 64
PAIR = 2 * RWKV_HEAD
CHUNK = 64
CONV_HALO = 32
XATTN_HEADS = 4
LN_EPS = 1e-5
GN_EPS = 64e-5
VMEM_LIMIT = 56 * 1024 * 1024

_NN = (((1,), (0,)), ((), ()))
_NT = (((1,), (1,)), ((), ()))
_TN = (((0,), (0,)), ((), ()))
_DONE = object()


def _params(*sem):
    return pltpu.CompilerParams(dimension_semantics=sem, vmem_limit_bytes=VMEM_LIMIT)


def _dot(a, b, dims=_NN):
    return lax.dot_general(a, b, dims, preferred_element_type=F32)


def _sigmoid(x):
    return 1.0 / (1.0 + jnp.exp(-x))


def _layer_norm_rows(v, g, b):
    mu = jnp.mean(v, axis=-1, keepdims=True)
    d = v - mu
    var = jnp.mean(d * d, axis=-1, keepdims=True)
    return d * lax.rsqrt(var + LN_EPS) * g + b


def _tile(n, pref):
    t = min(n, pref)
    assert n % t == 0, (n, pref)
    return t


def _mm_shift_kernel(x_ref, w_ref, mu_ref, o_ref, carry_ref, *, tiles_per_seq):
    m = pl.program_id(1)

    @pl.when(m % tiles_per_seq == 0)
    def _():
        carry_ref[...] = jnp.zeros_like(carry_ref)

    rows = min(x_ref.shape[0], 512)
    mu = mu_ref[...]
    prev_row = carry_ref[0:1, :]
    row = lax.broadcasted_iota(jnp.int32, (rows, o_ref.shape[1]), 0)
    for r0 in range(0, x_ref.shape[0], rows):
        z = _dot(x_ref[r0:r0 + rows, :], w_ref[...])
        prev = jnp.where(row == 0, prev_row, pltpu.roll(z, 1, axis=0))
        o_ref[r0:r0 + rows, :] = z + (prev - z) * mu
        prev_row = z[rows - 1:rows, :]
    carry_ref[0:1, :] = prev_row


def _mm_shift(x, w, mu, seq, tm=2048, tn=512):
    M, K = x.shape
    N = w.shape[1]
    tm = _tile(seq, tm)
    tn = _tile(N, tn)
    return pl.pallas_call(
        functools.partial(_mm_shift_kernel, tiles_per_seq=seq // tm),
        out_shape=jax.ShapeDtypeStruct((M, N), F32),
        grid=(N // tn, M // tm),
        in_specs=[
            pl.BlockSpec((tm, K), lambda n, m: (m, 0)),
            pl.BlockSpec((K, tn), lambda n, m: (0, n)),
            pl.BlockSpec((1, tn), lambda n, m: (0, n)),
        ],
        out_specs=pl.BlockSpec((tm, tn), lambda n, m: (m, n)),
        scratch_shapes=[pltpu.VMEM((8, tn), F32)],
        compiler_params=_params("parallel", "arbitrary"),
    )(x, w, mu)


def _mm_glu_kernel(x_ref, wa_ref, wb_ref, o_ref):
    rows = min(x_ref.shape[0], 256)
    for r0 in range(0, x_ref.shape[0], rows):
        x = x_ref[r0:r0 + rows, :]
        o_ref[r0:r0 + rows, :] = _dot(x, wa_ref[...]) * _sigmoid(_dot(x, wb_ref[...]))


def _mm_glu(x, wa, wb, tm=1024, tn=512):
    M, K = x.shape
    N = wa.shape[1]
    tm = _tile(M, tm)
    tn = _tile(N, tn)
    return pl.pallas_call(
        _mm_glu_kernel,
        out_shape=jax.ShapeDtypeStruct((M, N), F32),
        grid=(N // tn, M // tm),
        in_specs=[
            pl.BlockSpec((tm, K), lambda n, m: (m, 0)),
            pl.BlockSpec((K, tn), lambda n, m: (0, n)),
            pl.BlockSpec((K, tn), lambda n, m: (0, n)),
        ],
        out_specs=pl.BlockSpec((tm, tn), lambda n, m: (m, n)),
        compiler_params=_params("parallel", "arbitrary"),
    )(x, wa, wb)


def _mm_plain_kernel(x_ref, w_ref, o_ref):
    o_ref[...] = _dot(x_ref[...], w_ref[...]).astype(o_ref.dtype)


def _mm_plain(x, w, out_dtype, tm=1024, tn=512):
    M, K = x.shape
    N = w.shape[1]
    tm = _tile(M, tm)
    tn = _tile(N, tn)
    return pl.pallas_call(
        _mm_plain_kernel,
        out_shape=jax.ShapeDtypeStruct((M, N), out_dtype),
        grid=(N // tn, M // tm),
        in_specs=[
            pl.BlockSpec((tm, K), lambda n, m: (m, 0)),
            pl.BlockSpec((K, tn), lambda n, m: (0, n)),
        ],
        out_specs=pl.BlockSpec((tm, tn), lambda n, m: (m, n)),
        compiler_params=_params("parallel", "arbitrary"),
    )(x, w)


def _mm_merge_kernel(x_ref, orw_ref, ocv_ref, wgr_ref, wgc_ref, pr_ref, pc_ref, o_ref):
    x = x_ref[...]
    gate_r = _sigmoid(_dot(x, wgr_ref[...]))
    gate_c = _sigmoid(_dot(x, wgc_ref[...]))
    merged = gate_r * _dot(orw_ref[...], pr_ref[...]) + gate_c * _dot(ocv_ref[...], pc_ref[...])
    o_ref[...] = merged.astype(o_ref.dtype)


def _mm_merge(x, o_r, o_c, wgr, wgc, p_r, p_c, tm=512, tn=512):
    M, K = x.shape
    Kb = o_r.shape[1]
    N = wgr.shape[1]
    tm = _tile(M, tm)
    tn = _tile(N, tn)
    row = lambda k: pl.BlockSpec((tm, k), lambda n, m: (m, 0))
    col = lambda k: pl.BlockSpec((k, tn), lambda n, m: (0, n))
    return pl.pallas_call(
        _mm_merge_kernel,
        out_shape=jax.ShapeDtypeStruct((M, N), BF16),
        grid=(N // tn, M // tm),
        in_specs=[row(K), row(Kb), row(Kb), col(K), col(K), col(Kb), col(Kb)],
        out_specs=pl.BlockSpec((tm, tn), lambda n, m: (m, n)),
        compiler_params=_params("parallel", "arbitrary"),
    )(x, o_r, o_c, wgr, wgc, p_r, p_c)


def _ln_inplace(o_ref, ob_ref, g_ref, b_ref, rows=128):
    g = g_ref[...]
    b = b_ref[...]

    def body(i, _):
        sl = pl.ds(pl.multiple_of(i * rows, rows), rows)
        y = _layer_norm_rows(o_ref[sl, :], g, b)
        o_ref[sl, :] = y
        if ob_ref is not None:
            ob_ref[sl, :] = y.astype(ob_ref.dtype)
        return 0

    lax.fori_loop(0, o_ref.shape[0] // rows, body, 0)


def _mm_res_ln_kernel(y_ref, w_ref, res_ref, g_ref, b_ref, o_ref, ob_ref, *, alpha, tn):
    n = pl.program_id(1)
    col = pl.ds(pl.multiple_of(n * tn, tn), tn)
    o_ref[:, col] = alpha * res_ref[...] + _dot(y_ref[...], w_ref[...])

    @pl.when(n == pl.num_programs(1) - 1)
    def _():
        _ln_inplace(o_ref, ob_ref, g_ref, b_ref)


def _mm_res_ln(y, w, res, g, b, alpha, tm=512, tn=512):
    M, K = y.shape
    N = w.shape[1]
    tm = _tile(M, tm)
    tn = _tile(N, tn)
    return pl.pallas_call(
        functools.partial(_mm_res_ln_kernel, alpha=alpha, tn=tn),
        out_shape=(jax.ShapeDtypeStruct((M, N), F32), jax.ShapeDtypeStruct((M, N), BF16)),
        grid=(M // tm, N // tn),
        in_specs=[
            pl.BlockSpec((tm, K), lambda m, n: (m, 0)),
            pl.BlockSpec((K, tn), lambda m, n: (0, n)),
            pl.BlockSpec((tm, tn), lambda m, n: (m, n)),
            pl.BlockSpec((1, N), lambda m, n: (0, 0)),
            pl.BlockSpec((1, N), lambda m, n: (0, 0)),
        ],
        out_specs=(pl.BlockSpec((tm, N), lambda m, n: (m, 0)),
                   pl.BlockSpec((tm, N), lambda m, n: (m, 0))),
        compiler_params=_params("parallel", "arbitrary"),
    )(y, w.astype(BF16), res, g, b)


def _mlp_ln_kernel(x_ref, w1_ref, w2_ref, res_ref, g_ref, b_ref, o_ref, *, alpha, tn):
    k = pl.program_id(1)
    n_slabs = o_ref.shape[1] // tn
    rows = min(x_ref.shape[0], 512)

    def accumulate(first):
        for r0 in range(0, x_ref.shape[0], rows):
            rs = slice(r0, r0 + rows)
            h = jnp.maximum(_dot(x_ref[rs, :], w1_ref[...]), 0.0)
            h = (h * h).astype(BF16)
            for j in range(n_slabs):
                sl = slice(j * tn, (j + 1) * tn)
                if first:
                    o_ref[rs, sl] = _dot(h, w2_ref[:, sl])
                else:
                    o_ref[rs, sl] += _dot(h, w2_ref[:, sl])

    @pl.when(k == 0)
    def _():
        accumulate(True)

    @pl.when(k > 0)
    def _():
        accumulate(False)

    @pl.when(k < n_slabs)
    def _():
        col = pl.ds(pl.multiple_of(k * tn, tn), tn)
        o_ref[:, col] += alpha * res_ref[...]

    @pl.when(k == pl.num_programs(1) - 1)
    def _():
        _ln_inplace(o_ref, None, g_ref, b_ref)


def _mlp_ln(x, w1, w2, res, g, b, alpha, tm=1024, tk=512, tn=512):
    M, D = x.shape
    FF = w1.shape[1]
    tm = _tile(M, tm)
    tk = _tile(FF, tk)
    tn = _tile(D, tn)
    n_slabs = D // tn
    assert FF // tk >= n_slabs
    single = pl.Buffered(1)
    return pl.pallas_call(
        functools.partial(_mlp_ln_kernel, alpha=alpha, tn=tn),
        out_shape=jax.ShapeDtypeStruct((M, D), F32),
        grid=(M // tm, FF // tk),
        in_specs=[
            pl.BlockSpec((tm, D), lambda m, k: (m, 0), pipeline_mode=single),
            pl.BlockSpec((D, tk), lambda m, k: (0, k)),
            pl.BlockSpec((tk, D), lambda m, k: (k, 0)),
            pl.BlockSpec((tm, tn), lambda m, k: (m, jnp.minimum(k, n_slabs - 1))),
            pl.BlockSpec((1, D), lambda m, k: (0, 0)),
            pl.BlockSpec((1, D), lambda m, k: (0, 0)),
        ],
        out_specs=pl.BlockSpec((tm, D), lambda m, k: (m, 0), pipeline_mode=single),
        compiler_params=_params("parallel", "arbitrary"),
    )(x, w1.astype(BF16), w2.astype(BF16), res, g, b)


def _ln_rows_kernel(x_ref, g_ref, b_ref, o_ref):
    o_ref[...] = _layer_norm_rows(x_ref[...], g_ref[...], b_ref[...]).astype(o_ref.dtype)


def _ln_rows(x, g, b, out_dtype, tm=128):
    M, D = x.shape
    tm = _tile(M, tm)
    return pl.pallas_call(
        _ln_rows_kernel,
        out_shape=jax.ShapeDtypeStruct((M, D), out_dtype),
        grid=(M // tm,),
        in_specs=[pl.BlockSpec((tm, D), lambda m: (m, 0)),
                  pl.BlockSpec((1, D), lambda m: (0, 0)),
                  pl.BlockSpec((1, D), lambda m: (0, 0))],
        out_specs=pl.BlockSpec((tm, D), lambda m: (m, 0)),
        compiler_params=_params("parallel"),
    )(x, g, b)


def _lora_act_kernel(z_ref, o_ref):
    o_ref[:, 0:LANES] = jnp.tanh(z_ref[:, 0:LANES])
    o_ref[:, LANES:2 * LANES] = z_ref[:, LANES:2 * LANES]
    o_ref[:, 2 * LANES:] = _sigmoid(z_ref[:, 2 * LANES:])


def _lora_act(zs, col_block, tm=1024):
    M = zs.shape[0]
    W = 4 * LANES
    tm = _tile(M, tm)
    return pl.pallas_call(
        _lora_act_kernel,
        out_shape=jax.ShapeDtypeStruct((M, W), F32),
        grid=(M // tm,),
        in_specs=[pl.BlockSpec((tm, W), lambda m: (m, col_block))],
        out_specs=pl.BlockSpec((tm, W), lambda m: (m, 0)),
        compiler_params=_params("parallel"),
    )(zs)


def _rwkv_pair_stages(q, zr_ref, zk_ref, zv_ref, l_ref, wup_ref, aup_ref, gup_ref, prm_ref, ones_ref,
                      o_ref, h_ref, n_chunks):
    C = CHUNK
    lane = lax.broadcasted_iota(jnp.int32, (1, PAIR), 1)
    m0 = (lane < RWKV_HEAD).astype(F32)
    m1 = 1.0 - m0
    ri = lax.broadcasted_iota(jnp.int32, (PAIR, PAIR), 0)
    ci = lax.broadcasted_iota(jnp.int32, (PAIR, PAIR), 1)
    strict = ci < ri
    lower = ci <= ri
    eye = (ri == ci).astype(F32)
    head_ones = ones_ref[...]
    rc = lax.broadcasted_iota(jnp.int32, (C, C), 0)
    cc = lax.broadcasted_iota(jnp.int32, (C, C), 1)
    tril_c = (cc <= rc).astype(BF16)
    assert n_chunks % 2 == 0

    lanes = slice(q * PAIR, (q + 1) * PAIR)
    prm = prm_ref[:, lanes]
    w0, a0, k_k, k_a, r_k, gn_g, gn_b = (prm[i:i + 1, :] for i in range(7))
    w_up = wup_ref[:, lanes].astype(BF16)
    a_up = aup_ref[:, lanes].astype(BF16)
    g_up = gup_ref[:, lanes].astype(BF16)

    def stack(x):
        return jnp.concatenate([x * m0, x * m1], axis=0)

    m0_b = m0.astype(BF16)
    m1_b = m1.astype(BF16)

    def stack_b(x):
        x = x.astype(BF16)
        return jnp.concatenate([x * m0_b, x * m1_b], axis=0)

    def head_sum(x, passes):
        acc = None
        for _ in range(passes):
            piece = x.astype(BF16)
            part = _dot(piece, head_ones)
            acc = part if acc is None else acc + part
            x = x - piece.astype(F32)
        return acc

    def bdot(a, b, dims=_NN):
        return _dot(a.astype(BF16), b.astype(BF16), dims)

    r = zr_ref[:, lanes]
    k = zk_ref[:, lanes]
    v = zv_ref[:, lanes]
    lora = l_ref[...].astype(BF16)
    w_raw = w0 + _dot(lora[:, 0:LANES], w_up)
    a_raw = a0 + _dot(lora[:, LANES:2 * LANES], a_up)
    g = _dot(lora[:, 2 * LANES:], g_up)
    wl = -math.exp(-0.5) * _sigmoid(w_raw)
    a_sig = _sigmoid(a_raw)
    kk = k * k_k
    kk = kk * lax.rsqrt(jnp.maximum(head_sum(kk * kk, 1), 1e-24))
    k2 = k * (1.0 + (a_sig - 1.0) * k_a)
    b_s = kk * a_sig
    yield

    chunks = range(n_chunks)
    rows = [slice(c * C, (c + 1) * C) for c in chunks]
    wl_hi = wl.astype(BF16)
    wl_r = wl - wl_hi.astype(F32)
    wl_mid = wl_r.astype(BF16)
    wl_lo = (wl_r - wl_mid.astype(F32)).astype(BF16)
    wl_parts = jnp.concatenate([wl_hi, wl_mid, wl_lo], axis=1)
    cum_parts = [_dot(tril_c, wl_parts[s]) for s in rows]
    cum_c = [m[:, :PAIR] + (m[:, PAIR:2 * PAIR] + m[:, 2 * PAIR:]) for m in cum_parts]
    cum = jnp.concatenate(cum_c, axis=0)
    cum_end = jnp.concatenate([jnp.broadcast_to(m[C - 1:C, :], (C, PAIR)) for m in cum_c], axis=0)
    e_neg = jnp.exp(-cum)
    e_tail = jnp.exp(cum_end - cum)
    r_t = r * jnp.exp(cum)
    a_t = -kk * jnp.exp(cum - wl)
    b_t = b_s * e_neg
    k_t = k2 * e_neg
    bw = b_s * e_tail
    kw = k2 * e_tail
    w_end = jnp.exp(cum_end)
    yield

    a_st = [stack_b(a_t[s]) for s in rows]
    r_st = [stack(r_t[s]) for s in rows]
    v_st = [stack_b(v[s]) for s in rows]
    gram = [bdot(jnp.concatenate([a_st[c], r_st[c].astype(BF16)], axis=0),
                 jnp.concatenate([stack_b(b_t[rows[c]]), stack_b(k_t[rows[c]])], axis=0), _NT)
            for c in chunks]
    yield
    a_ab =[jnp.where(strict, m[:PAIR, :PAIR], 0.0) for m in gram]
    a_ak = [jnp.where(strict, m[:PAIR, PAIR:], 0.0).astype(BF16) for m in gram]
    g_rb = [jnp.where(lower, m[PAIR:, :PAIR], 0.0).astype(BF16) for m in gram]
    g_rk = [jnp.where(lower, m[PAIR:, PAIR:], 0.0).astype(BF16) for m in gram]

    pw = [m.astype(BF16) for m in a_ab]
    pw = [_dot(m, m).astype(BF16) for m in pw]
    t_inv = [eye + m for m in a_ab]
    yield
    n = 2
    while 2 * n < C:
        both = [_dot(pw[c], jnp.concatenate([pw[c], t_inv[c].astype(BF16)], axis=1)) for c in chunks]
        pw = [m[:, :PAIR].astype(BF16) for m in both]
        t_inv = [t_inv[c] + both[c][:, PAIR:] for c in chunks]
        n *= 2
        yield
    t_b = [(t_inv[c] + _dot(pw[c], t_inv[c].astype(BF16))).astype(BF16) for c in chunks]

    a_v = [_dot(a_ak[c], v_st[c]).astype(BF16) for c in chunks]
    yield
    t_av =[_dot(t_b[c], jnp.concatenate([a_st[c], a_v[c]], axis=1)).astype(BF16)
            for c in chunks]
    zero = jnp.zeros((PAIR, PAIR), BF16)
    rhs2 = [jnp.concatenate([t_av[c], jnp.concatenate([zero, v_st[c]], axis=1)], axis=0)
            for c in chunks]
    yield
    bw_st = [stack_b(bw[s]) for s in rows]
    kw_st = [stack_b(kw[s]) for s in rows]
    mn = [_dot(jnp.concatenate([bw_st[c], kw_st[c]], axis=0), rhs2[c], _TN) for c in chunks]
    qz = [_dot(jnp.concatenate([g_rb[c], g_rk[c]], axis=1), rhs2[c]) for c in chunks]
    m_c = [(eye * w_end[rows[c]][0:1, :] + mn[c][:, :PAIR]).astype(BF16) for c in chunks]
    n_c = [mn[c][:, PAIR:] for c in chunks]
    q_c = [(r_st[c] + qz[c][:, :PAIR]).astype(BF16) for c in chunks]
    z_c = [qz[c][:, PAIR:] for c in chunks]
    yield

    pairs = range(n_chunks // 2)
    comp = [_dot(m_c[2 * j + 1], jnp.concatenate([m_c[2 * j], n_c[2 * j].astype(BF16)], axis=1))
            for j in pairs]
    m_2 = [comp[j][:, :PAIR].astype(BF16) for j in pairs]
    n_2 = [comp[j][:, PAIR:] + n_c[2 * j + 1] for j in pairs]
    yield
    h = h_ref[q]
    outs = []
    for j in pairs:
        c0 = 2 * j
        c1 = c0 + 1
        h_b = h.astype(BF16)
        from_h = _dot(jnp.concatenate([m_2[j], m_c[c0], q_c[c0]], axis=0), h_b)
        h = from_h[:PAIR, :] + n_2[j]
        h_mid = (from_h[PAIR:2 * PAIR, :] + n_c[c0]).astype(BF16)
        o_0 = from_h[2 * PAIR:, :] + z_c[c0]
        o_1 = _dot(q_c[c1], h_mid) + z_c[c1]
        outs.append(o_0[:C, :] + o_0[C:, :])
        outs.append(o_1[:C, :] + o_1[C:, :])
        yield
    h_ref[q] = h
    o = jnp.concatenate(outs, axis=0)

    inv_n = 1.0 / RWKV_HEAD
    mu = head_sum(o, 2) * inv_n
    d = o - mu
    var = head_sum(d * d, 2) * inv_n
    y = d * lax.rsqrt(var + GN_EPS) * gn_g + gn_b
    y = y + head_sum(r * k2 * r_k, 1) * v
    o_ref[:, lanes] = (y * g).astype(o_ref.dtype)


def _rwkv_kernel(*refs, n_chunks):
    o_ref, h_ref = refs[-2:]

    @pl.when(pl.program_id(2) == 0)
    def _():
        h_ref[...] = jnp.zeros_like(h_ref)

    live = []
    for q in range(o_ref.shape[1] // PAIR):
        live.append(_rwkv_pair_stages(q, *refs, n_chunks))
        for _ in range(3):
            live = [gen for gen in live if next(gen, _DONE) is not _DONE]
    while live:
        live = [gen for gen in live if next(gen, _DONE) is not _DONE]


def _rwkv_mix(zs, lora, w_up, a_up, g_up, prm, batch, seq, d_rwkv, tt=1024, pairs_per_step=2):
    M = zs.shape[0]
    wb = pairs_per_step * PAIR
    n_cols = d_rwkv // wb
    tt = _tile(seq, tt)
    nt = seq // tt
    zspec = lambda off: pl.BlockSpec((tt, wb), lambda b, p, t: (b * nt + t, off + p))
    wspec = lambda k: pl.BlockSpec((k, wb), lambda b, p, t: (0, p))
    head_of = jnp.arange(PAIR) // RWKV_HEAD
    same_head = (head_of[:, None] == head_of[None, :]).astype(BF16)
    return pl.pallas_call(
        functools.partial(_rwkv_kernel, n_chunks=tt // CHUNK),
        out_shape=jax.ShapeDtypeStruct((M, d_rwkv), BF16),
        grid=(batch, n_cols, nt),
        in_specs=[
            zspec(0), zspec(n_cols), zspec(2 * n_cols),
            pl.BlockSpec((tt, 4 * LANES), lambda b, p, t: (b * nt + t, 0)),
            wspec(LANES), wspec(LANES), wspec(2 * LANES), wspec(8),
            pl.BlockSpec((PAIR, PAIR), lambda b, p, t: (0, 0)),
        ],
        out_specs=pl.BlockSpec((tt, wb), lambda b, p, t: (b * nt + t, p)),
        scratch_shapes=[pltpu.VMEM((pairs_per_step, PAIR, PAIR), F32)],
        compiler_params=_params("parallel", "parallel", "arbitrary"),
    )(zs, zs, zs, lora, w_up, a_up, g_up, prm, same_head)


def _conv_kernel(u_ref, w_ref, cb_ref, g_ref, b_ref, o_ref, buf_ref, acc_ref, win_ref, *, width):
    tt, ch = u_ref.shape
    t = pl.program_id(1)

    @pl.when(t == 0)
    def _():
        buf_ref[0:CONV_HALO, :] = jnp.zeros((CONV_HALO, ch), F32)

    buf_ref[CONV_HALO:, :] = u_ref[...]
    base = CONV_HALO - (width - 1)

    sub = 8

    def slab(s, _):
        cols = pl.ds(pl.multiple_of(s * LANES, LANES), LANES)
        acc = jnp.broadcast_to(cb_ref[:, cols], (tt, LANES))
        for res in range(sub):
            taps = [j for j in range(width) if (base + j) % sub == res]
            if not taps:
                continue
            span = (base + taps[-1]) // sub * sub
            win_ref[0:tt + span, :] = buf_ref[pl.ds(res, tt + span), cols]
            for j in taps:
                off = base + j - res
                acc = acc + w_ref[pl.ds(j, 1), cols] * win_ref[off:off + tt, :]
        acc_ref[:, cols] = acc
        return 0

    lax.fori_loop(0, ch // LANES, slab, 0)
    buf_ref[0:CONV_HALO, :] = buf_ref[tt:tt + CONV_HALO, :]

    g = g_ref[...]
    b = b_ref[...]
    rows = min(tt, 128)

    def norm(i, _):
        sl = pl.ds(pl.multiple_of(i * rows, rows), rows)
        y = _layer_norm_rows(acc_ref[sl, :], g, b)
        o_ref[sl, :] = (y * _sigmoid(y)).astype(o_ref.dtype)
        return 0

    lax.fori_loop(0, tt // rows, norm, 0)


def _conv_module(u, conv_w, conv_b, ln_g, ln_b, batch, seq, width, tt=256):
    M, ch = u.shape
    tt = _tile(seq, tt)
    nt = seq // tt
    vec = lambda r: pl.BlockSpec((r, ch), lambda b, t: (0, 0))
    return pl.pallas_call(
        functools.partial(_conv_kernel, width=width),
        out_shape=jax.ShapeDtypeStruct((M, ch), BF16),
        grid=(batch, nt),
        in_specs=[pl.BlockSpec((tt, ch), lambda b, t: (b * nt + t, 0)),
                  vec(conv_w.shape[0]), vec(1), vec(1), vec(1)],
        out_specs=pl.BlockSpec((tt, ch), lambda b, t: (b * nt + t, 0)),
        scratch_shapes=[pltpu.VMEM((tt + CONV_HALO, ch), F32), pltpu.VMEM((tt, ch), F32),
                        pltpu.VMEM((tt + CONV_HALO, LANES), F32)],
        compiler_params=_params("parallel", "arbitrary"),
    )(u, conv_w, conv_b, ln_g, ln_b)


def _qk_fold_kernel(wq_ref, k_ref, o_ref):
    o_ref[...] = _dot(wq_ref[...], k_ref[...], _NT).astype(o_ref.dtype)


def _qk_fold(wq, kx, batch, mem_len, heads):
    D = wq.shape[0]
    hd = D // heads
    return pl.pallas_call(
        _qk_fold_kernel,
        out_shape=jax.ShapeDtypeStruct((batch, D, heads * mem_len), BF16),
        grid=(heads, batch),
        in_specs=[pl.BlockSpec((D, hd), lambda h, b: (0, h)),
                  pl.BlockSpec((mem_len, hd), lambda h, b: (b, h))],
        out_specs=pl.BlockSpec((None, D, mem_len), lambda h, b: (b, 0, h)),
        compiler_params=_params("parallel", "arbitrary"),
    )(wq, kx)


def _vo_fold_kernel(v_ref, wo_ref, o_ref):
    o_ref[...] = _dot(v_ref[...], wo_ref[...]).astype(o_ref.dtype)


def _vo_fold(vx, wo, batch, mem_len, heads):
    D = wo.shape[0]
    hd = D // heads
    return pl.pallas_call(
        _vo_fold_kernel,
        out_shape=jax.ShapeDtypeStruct((batch, heads * mem_len, D), BF16),
        grid=(heads, batch),
        in_specs=[pl.BlockSpec((mem_len, hd), lambda h, b: (b, h)),
                  pl.BlockSpec((hd, D), lambda h, b: (h, 0))],
        out_specs=pl.BlockSpec((None, mem_len, D), lambda h, b: (b, h, 0)),
        compiler_params=_params("parallel", "arbitrary"),
    )(vx, wo)


def _attn_ln_kernel(x_ref, qk_ref, vo_ref, res_ref, g_ref, b_ref, o_ref, ob_ref, *, heads, scale, alpha):
    s = _dot(x_ref[...], qk_ref[...]) * scale
    mem_len = s.shape[1] // heads
    probs = []
    for h in range(heads):
        sh = s[:, h * mem_len:(h + 1) * mem_len]
        e = jnp.exp(sh - jnp.max(sh, axis=-1, keepdims=True))
        probs.append((e / jnp.sum(e, axis=-1, keepdims=True)).astype(BF16))
    p = jnp.concatenate(probs, axis=1)
    o_ref[...] = alpha * res_ref[...] + _dot(p, vo_ref[...])
    _ln_inplace(o_ref, ob_ref, g_ref, b_ref)


def _attn_ln(x, qk, vo, res, g, b, alpha, batch, seq, heads, tq=256):
    M, D = x.shape
    hm = qk.shape[2]
    tq = _tile(seq, tq)
    nt = seq // tq
    scale = float((D // heads) ** -0.5)
    single = pl.Buffered(1)
    tok = lambda: pl.BlockSpec((tq, D), lambda bi, t: (bi * nt + t, 0))
    vec = pl.BlockSpec((1, D), lambda bi, t: (0, 0))
    return pl.pallas_call(
        functools.partial(_attn_ln_kernel, heads=heads, scale=scale, alpha=alpha),
        out_shape=(jax.ShapeDtypeStruct((M, D), F32), jax.ShapeDtypeStruct((M, D), BF16)),
        grid=(batch, nt),
        in_specs=[tok(),
                  pl.BlockSpec((None, D, hm), lambda bi, t: (bi, 0, 0), pipeline_mode=single),
                  pl.BlockSpec((None, hm, D), lambda bi, t: (bi, 0, 0), pipeline_mode=single),
                  tok(), vec, vec],
        out_specs=(tok(), tok()),
        compiler_params=_params("parallel", "arbitrary"),
    )(x, qk, vo, res, g, b)


def _pad_cols(w, n):
    return jnp.pad(w, ((0, 0), (0, n - w.shape[1])))


def _pad_rows(w, n):
    return jnp.pad(w, ((0, n - w.shape[0]), (0, 0)))


def kernel(x, mem, w_in, rwkv_shift_mix, rwkv_w0, rwkv_w_up, rwkv_a0, rwkv_a_up, rwkv_g_up, rwkv_k_k, rwkv_k_a, rwkv_r_k, rwkv_gn_g, rwkv_gn_b, conv_w, conv_b, conv_ln_g, conv_ln_b, proj_rwkv, proj_conv, w_out, ln1_g, ln1_b, ln_mem_g, ln_mem_b, xattn_wq, xattn_wk, xattn_wv, xattn_wo, ln2_g, ln2_b, mlp_w1, mlp_w2, ln3_g, ln3_b):
    B, S, D = x.shape
    depth = w_in.shape[0]
    mem_len = mem.shape[1]
    d_rwkv = rwkv_w0.shape[1]
    d_conv = conv_b.shape[1]
    r_decay = rwkv_w_up.shape[1]
    r_iclr = rwkv_a_up.shape[1]
    r_gate = rwkv_g_up.shape[1]
    width = conv_w.shape[1]
    assert r_decay <= LANES and r_iclr <= LANES and r_gate == 2 * LANES
    assert d_rwkv % PAIR == 0 and width - 1 <= CONV_HALO
    alpha = float((2 * depth) ** 0.25)
    M = B * S
    row = lambda a: a.reshape(1, -1)

    mem_n = _ln_rows(mem.reshape(B * mem_len, D), row(ln_mem_g), row(ln_mem_b), BF16)
    h = x.reshape(M, D)
    hb = h.astype(BF16)
    for l in range(depth):
        w = w_in[l]
        c_lora = 3 * d_rwkv
        c_conv = c_lora + r_decay + r_iclr + r_gate
        c_gate = c_conv + 2 * d_conv
        mix = rwkv_shift_mix[l]
        zs = _mm_shift(hb, w[:, :c_lora].astype(BF16), row(mix[:c_lora]), S)
        w_lora = jnp.concatenate([
            _pad_cols(w[:, c_lora:c_lora + r_decay], LANES),
            _pad_cols(w[:, c_lora + r_decay:c_lora + r_decay + r_iclr], LANES),
            w[:, c_lora + r_decay + r_iclr:c_conv]], axis=1).astype(BF16)
        mu_lora = jnp.concatenate([
            jnp.pad(mix[c_lora:c_lora + r_decay], (0, LANES - r_decay)),
            jnp.pad(mix[c_lora + r_decay:c_lora + r_decay + r_iclr], (0, LANES - r_iclr)),
            mix[c_lora + r_decay + r_iclr:]]).reshape(1, -1)
        lora = _lora_act(_mm_shift(hb, w_lora, mu_lora, S), 0)
        prm = jnp.stack([rwkv_w0[l], rwkv_a0[l], rwkv_k_k[l], rwkv_k_a[l], rwkv_r_k[l].reshape(-1),
                         rwkv_gn_g[l], rwkv_gn_b[l], jnp.zeros((d_rwkv,), F32)])
        o_r = _rwkv_mix(zs, lora, _pad_rows(rwkv_w_up[l], LANES), _pad_rows(rwkv_a_up[l], LANES),
                        rwkv_g_up[l], prm, B, S, d_rwkv)

        u = _mm_glu(hb, w[:, c_conv:c_conv + d_conv].astype(BF16),
                    w[:, c_conv + d_conv:c_gate].astype(BF16))
        cw = _pad_rows(conv_w[l].reshape(width, d_conv), CONV_HALO)
        o_c = _conv_module(u, cw, row(conv_b[l]), row(conv_ln_g[l]), row(conv_ln_b[l]), B, S, width)

        merged = _mm_merge(hb, o_r, o_c, w[:, c_gate:c_gate + D].astype(BF16),
                           w[:, c_gate + D:].astype(BF16),
                           proj_rwkv[l].astype(BF16), proj_conv[l].astype(BF16))
        h, hb = _mm_res_ln(merged, w_out[l], h, row(ln1_g[l]), row(ln1_b[l]), alpha)

        kx = _mm_plain(mem_n, xattn_wk[l].astype(BF16), BF16)
        vx = _mm_plain(mem_n, xattn_wv[l].astype(BF16), BF16)
        qk = _qk_fold(xattn_wq[l].astype(BF16), kx, B, mem_len, XATTN_HEADS)
        vo = _vo_fold(vx, xattn_wo[l].astype(BF16), B, mem_len, XATTN_HEADS)
        h, hb = _attn_ln(hb, qk, vo, h, row(ln2_g[l]), row(ln2_b[l]), alpha, B, S, XATTN_HEADS)

        h = _mlp_ln(hb, mlp_w1[l], mlp_w2[l], h, row(ln3_g[l]), row(ln3_b[l]), alpha)
        hb = h.astype(BF16)
    return h.reshape(B, S, D)
```

```python
import functools
import math

import jax
import jax.numpy as jnp
from jax import lax
from jax.experimental import pallas as pl
from jax.experimental.pallas import tpu as pltpu

F32 = jnp.float32
BF16 = jnp.bfloat16

LANES = 128
RWKV_HEAD = 64
PAIR = 2 * RWKV_HEAD
CHUNK = 64
CONV_HALO = 32
XATTN_HEADS = 4
LN_EPS = 1e-5
GN_EPS = 64e-5
VMEM_LIMIT = 56 * 1024 * 1024

_NN = (((1,), (0,)), ((), ()))
_NT = (((1,), (1,)), ((), ()))
_TN = (((0,), (0,)), ((), ()))
_DONE = object()


def _params(*sem):
    return pltpu.CompilerParams(dimension_semantics=sem, vmem_limit_bytes=VMEM_LIMIT)


def _dot(a, b, dims=_NN):
    return lax.dot_general(a, b, dims, preferred_element_type=F32)


def _sigmoid(x):
    return 1.0 / (1.0 + jnp.exp(-x))


def _layer_norm_rows(v, g, b):
    mu = jnp.mean(v, axis=-1, keepdims=True)
    d = v - mu
    var = jnp.mean(d * d, axis=-1, keepdims=True)
    return d * lax.rsqrt(var + LN_EPS) * g + b


def _tile(n, pref):
    t = min(n, pref)
    assert n % t == 0, (n, pref)
    return t


def _mm_shift_kernel(x_ref, w_ref, mu_ref, o_ref, carry_ref, *, tiles_per_seq):
    m = pl.program_id(1)

    @pl.when(m % tiles_per_seq == 0)
    def _():
        carry_ref[...] = jnp.zeros_like(carry_ref)

    rows = min(x_ref.shape[0], 512)
    mu = mu_ref[...]
    prev_row = carry_ref[0:1, :]
    row = lax.broadcasted_iota(jnp.int32, (rows, o_ref.shape[1]), 0)
    for r0 in range(0, x_ref.shape[0], rows):
        z = _dot(x_ref[r0:r0 + rows, :], w_ref[...])
        prev = jnp.where(row == 0, prev_row, pltpu.roll(z, 1, axis=0))
        o_ref[r0:r0 + rows, :] = z + (prev - z) * mu
        prev_row = z[rows - 1:rows, :]
    carry_ref[0:1, :] = prev_row


def _mm_shift(x, w, mu, seq, tm=2048, tn=512):
    M, K = x.shape
    N = w.shape[1]
    tm = _tile(seq, tm)
    tn = _tile(N, tn)
    return pl.pallas_call(
        functools.partial(_mm_shift_kernel, tiles_per_seq=seq // tm),
        out_shape=jax.ShapeDtypeStruct((M, N), F32),
        grid=(N // tn, M // tm),
        in_specs=[
            pl.BlockSpec((tm, K), lambda n, m: (m, 0)),
            pl.BlockSpec((K, tn), lambda n, m: (0, n)),
            pl.BlockSpec((1, tn), lambda n, m: (0, n)),
        ],
        out_specs=pl.BlockSpec((tm, tn), lambda n, m: (m, n)),
        scratch_shapes=[pltpu.VMEM((8, tn), F32)],
        compiler_params=_params("parallel", "arbitrary"),
    )(x, w, mu)


def _mm_glu_kernel(x_ref, wa_ref, wb_ref, o_ref):
    rows = min(x_ref.shape[0], 256)
    for r0 in range(0, x_ref.shape[0], rows):
        x = x_ref[r0:r0 + rows, :]
        o_ref[r0:r0 + rows, :] = _dot(x, wa_ref[...]) * _sigmoid(_dot(x, wb_ref[...]))


def _mm_glu(x, wa, wb, tm=1024, tn=512):
    M, K = x.shape
    N = wa.shape[1]
    tm = _tile(M, tm)
    tn = _tile(N, tn)
    return pl.pallas_call(
        _mm_glu_kernel,
        out_shape=jax.ShapeDtypeStruct((M, N), F32),
        grid=(N // tn, M // tm),
        in_specs=[
            pl.BlockSpec((tm, K), lambda n, m: (m, 0)),
            pl.BlockSpec((K, tn), lambda n, m: (0, n)),
            pl.BlockSpec((K, tn), lambda n, m: (0, n)),
        ],
        out_specs=pl.BlockSpec((tm, tn), lambda n, m: (m, n)),
        compiler_params=_params("parallel", "arbitrary"),
    )(x, wa, wb)


def _mm_plain_kernel(x_ref, w_ref, o_ref):
    o_ref[...] = _dot(x_ref[...], w_ref[...]).astype(o_ref.dtype)


def _mm_plain(x, w, out_dtype, tm=1024, tn=512):
    M, K = x.shape
    N = w.shape[1]
    tm = _tile(M, tm)
    tn = _tile(N, tn)
    return pl.pallas_call(
        _mm_plain_kernel,
        out_shape=jax.ShapeDtypeStruct((M, N), out_dtype),
        grid=(N // tn, M // tm),
        in_specs=[
            pl.BlockSpec((tm, K), lambda n, m: (m, 0)),
            pl.BlockSpec((K, tn), lambda n, m: (0, n)),
        ],
        out_specs=pl.BlockSpec((tm, tn), lambda n, m: (m, n)),
        compiler_params=_params("parallel", "arbitrary"),
    )(x, w)


def _mm_merge_kernel(x_ref, orw_ref, ocv_ref, wgr_ref, wgc_ref, pr_ref, pc_ref, o_ref):
    x = x_ref[...]
    gate_r = _sigmoid(_dot(x, wgr_ref[...]))
    gate_c = _sigmoid(_dot(x, wgc_ref[...]))
    merged = gate_r * _dot(orw_ref[...], pr_ref[...]) + gate_c * _dot(ocv_ref[...], pc_ref[...])
    o_ref[...] = merged.astype(o_ref.dtype)


def _mm_merge(x, o_r, o_c, wgr, wgc, p_r, p_c, tm=512, tn=512):
    M, K = x.shape
    Kb = o_r.shape[1]
    N = wgr.shape[1]
    tm = _tile(M, tm)
    tn = _tile(N, tn)
    row = lambda k: pl.BlockSpec((tm, k), lambda n, m: (m, 0))
    col = lambda k: pl.BlockSpec((k, tn), lambda n, m: (0, n))
    return pl.pallas_call(
        _mm_merge_kernel,
        out_shape=jax.ShapeDtypeStruct((M, N), BF16),
        grid=(N // tn, M // tm),
        in_specs=[row(K), row(Kb), row(Kb), col(K), col(K), col(Kb), col(Kb)],
        out_specs=pl.BlockSpec((tm, tn), lambda n, m: (m, n)),
        compiler_params=_params("parallel", "arbitrary"),
    )(x, o_r, o_c, wgr, wgc, p_r, p_c)


def _ln_inplace(o_ref, ob_ref, g_ref, b_ref, rows=128):
    g = g_ref[...]
    b = b_ref[...]

    def body(i, _):
        sl = pl.ds(pl.multiple_of(i * rows, rows), rows)
        y = _layer_norm_rows(o_ref[sl, :], g, b)
        o_ref[sl, :] = y
        if ob_ref is not None:
            ob_ref[sl, :] = y.astype(ob_ref.dtype)
        return 0

    lax.fori_loop(0, o_ref.shape[0] // rows, body, 0)


def _mm_res_ln_kernel(y_ref, w_ref, res_ref, g_ref, b_ref, o_ref, ob_ref, *, alpha, tn):
    n = pl.program_id(1)
    col = pl.ds(pl.multiple_of(n * tn, tn), tn)
    o_ref[:, col] = alpha * res_ref[...] + _dot(y_ref[...], w_ref[...])

    @pl.when(n == pl.num_programs(1) - 1)
    def _():
        _ln_inplace(o_ref, ob_ref, g_ref, b_ref)


def _mm_res_ln(y, w, res, g, b, alpha, tm=512, tn=512):
    M, K = y.shape
    N = w.shape[1]
    tm = _tile(M, tm)
    tn = _tile(N, tn)
    return pl.pallas_call(
        functools.partial(_mm_res_ln_kernel, alpha=alpha, tn=tn),
        out_shape=(jax.ShapeDtypeStruct((M, N), F32), jax.ShapeDtypeStruct((M, N), BF16)),
        grid=(M // tm, N // tn),
        in_specs=[
            pl.BlockSpec((tm, K), lambda m, n: (m, 0)),
            pl.BlockSpec((K, tn), lambda m, n: (0, n)),
            pl.BlockSpec((tm, tn), lambda m, n: (m, n)),
            pl.BlockSpec((1, N), lambda m, n: (0, 0)),
            pl.BlockSpec((1, N), lambda m, n: (0, 0)),
        ],
        out_specs=(pl.BlockSpec((tm, N), lambda m, n: (m, 0)),
                   pl.BlockSpec((tm, N), lambda m, n: (m, 0))),
        compiler_params=_params("parallel", "arbitrary"),
    )(y, w.astype(BF16), res, g, b)


def _mlp_ln_kernel(x_ref, w1_ref, w2_ref, res_ref, g_ref, b_ref, o_ref, *, alpha, tn):
    k = pl.program_id(1)
    n_slabs = o_ref.shape[1] // tn
    rows = min(x_ref.shape[0], 512)

    def accumulate(first):
        for r0 in range(0, x_ref.shape[0], rows):
            rs = slice(r0, r0 + rows)
            h = jnp.maximum(_dot(x_ref[rs, :], w1_ref[...]), 0.0)
            h = (h * h).astype(BF16)
            for j in range(n_slabs):
                sl = slice(j * tn, (j + 1) * tn)
                if first:
                    o_ref[rs, sl] = _dot(h, w2_ref[:, sl])
                else:
                    o_ref[rs, sl] += _dot(h, w2_ref[:, sl])

    @pl.when(k == 0)
    def _():
        accumulate(True)

    @pl.when(k > 0)
    def _():
        accumulate(False)

    @pl.when(k < n_slabs)
    def _():
        col = pl.ds(pl.multiple_of(k * tn, tn), tn)
        o_ref[:, col] += alpha * res_ref[...]

    @pl.when(k == pl.num_programs(1) - 1)
    def _():
        _ln_inplace(o_ref, None, g_ref, b_ref)


def _mlp_ln(x, w1, w2, res, g, b, alpha, tm=1024, tk=512, tn=512):
    M, D = x.shape
    FF = w1.shape[1]
    tm = _tile(M, tm)
    tk = _tile(FF, tk)
    tn = _tile(D, tn)
    n_slabs = D // tn
    assert FF // tk >= n_slabs
    single = pl.Buffered(1)
    return pl.pallas_call(
        functools.partial(_mlp_ln_kernel, alpha=alpha, tn=tn),
        out_shape=jax.ShapeDtypeStruct((M, D), F32),
        grid=(M // tm, FF // tk),
        in_specs=[
            pl.BlockSpec((tm, D), lambda m, k: (m, 0)),
            pl.BlockSpec((D, tk), lambda m, k: (0, k)),
            pl.BlockSpec((tk, D), lambda m, k: (k, 0)),
            pl.BlockSpec((tm, tn), lambda m, k: (m, jnp.minimum(k, n_slabs - 1))),
            pl.BlockSpec((1, D), lambda m, k: (0, 0)),
            pl.BlockSpec((1, D), lambda m, k: (0, 0)),
        ],
        out_specs=pl.BlockSpec((tm, D), lambda m, k: (m, 0), pipeline_mode=single),
        compiler_params=_params("parallel", "arbitrary"),
    )(x, w1.astype(BF16), w2.astype(BF16), res, g, b)


def _ln_rows_kernel(x_ref, g_ref, b_ref, o_ref):
    o_ref[...] = _layer_norm_rows(x_ref[...], g_ref[...], b_ref[...]).astype(o_ref.dtype)


def _ln_rows(x, g, b, out_dtype, tm=128):
    M, D = x.shape
    tm = _tile(M, tm)
    return pl.pallas_call(
        _ln_rows_kernel,
        out_shape=jax.ShapeDtypeStruct((M, D), out_dtype),
        grid=(M // tm,),
        in_specs=[pl.BlockSpec((tm, D), lambda m: (m, 0)),
                  pl.BlockSpec((1, D), lambda m: (0, 0)),
                  pl.BlockSpec((1, D), lambda m: (0, 0))],
        out_specs=pl.BlockSpec((tm, D), lambda m: (m, 0)),
        compiler_params=_params("parallel"),
    )(x, g, b)


def _lora_act_kernel(z_ref, o_ref):
    o_ref[:, 0:LANES] = jnp.tanh(z_ref[:, 0:LANES])
    o_ref[:, LANES:2 * LANES] = z_ref[:, LANES:2 * LANES]
    o_ref[:, 2 * LANES:] = _sigmoid(z_ref[:, 2 * LANES:])


def _lora_act(zs, col_block, tm=1024):
    M = zs.shape[0]
    W = 4 * LANES
    tm = _tile(M, tm)
    return pl.pallas_call(
        _lora_act_kernel,
        out_shape=jax.ShapeDtypeStruct((M, W), F32),
        grid=(M // tm,),
        in_specs=[pl.BlockSpec((tm, W), lambda m: (m, col_block))],
        out_specs=pl.BlockSpec((tm, W), lambda m: (m, 0)),
        compiler_params=_params("parallel"),
    )(zs)


def _rwkv_pair_stages(q, zr_ref, zk_ref, zv_ref, l_ref, wup_ref, aup_ref, gup_ref, prm_ref, ones_ref,
                      o_ref, h_ref, n_chunks):
    C = CHUNK
    lane = lax.broadcasted_iota(jnp.int32, (1, PAIR), 1)
    m0 = (lane < RWKV_HEAD).astype(F32)
    m1 = 1.0 - m0
    ri = lax.broadcasted_iota(jnp.int32, (PAIR, PAIR), 0)
    ci = lax.broadcasted_iota(jnp.int32, (PAIR, PAIR), 1)
    strict = ci < ri
    lower = ci <= ri
    eye = (ri == ci).astype(F32)
    head_ones = ones_ref[...]
    rc = lax.broadcasted_iota(jnp.int32, (C, C), 0)
    cc = lax.broadcasted_iota(jnp.int32, (C, C), 1)
    tril_c = (cc <= rc).astype(BF16)
    assert n_chunks % 2 == 0

    lanes = slice(q * PAIR, (q + 1) * PAIR)
    prm = prm_ref[:, lanes]
    w0, a0, k_k, k_a, r_k, gn_g, gn_b = (prm[i:i + 1, :] for i in range(7))
    w_up = wup_ref[:, lanes].astype(BF16)
    a_up = aup_ref[:, lanes].astype(BF16)
    g_up = gup_ref[:, lanes].astype(BF16)

    def stack(x):
        return jnp.concatenate([x * m0, x * m1], axis=0)

    m0_b = m0.astype(BF16)
    m1_b = m1.astype(BF16)

    def stack_b(x):
        x = x.astype(BF16)
        return jnp.concatenate([x * m0_b, x * m1_b], axis=0)

    def head_sum(x, passes):
        acc = None
        for _ in range(passes):
            piece = x.astype(BF16)
            part = _dot(piece, head_ones)
            acc = part if acc is None else acc + part
            x = x - piece.astype(F32)
        return acc

    def bdot(a, b, dims=_NN):
        return _dot(a.astype(BF16), b.astype(BF16), dims)

    r = zr_ref[:, lanes]
    k = zk_ref[:, lanes]
    v = zv_ref[:, lanes]
    lora = l_ref[...].astype(BF16)
    w_raw = w0 + _dot(lora[:, 0:LANES], w_up)
    a_raw = a0 + _dot(lora[:, LANES:2 * LANES], a_up)
    g = _dot(lora[:, 2 * LANES:], g_up)
    wl = -math.exp(-0.5) * _sigmoid(w_raw)
    a_sig = _sigmoid(a_raw)
    kk = k * k_k
    kk = kk * lax.rsqrt(jnp.maximum(head_sum(kk * kk, 1), 1e-24))
    k2 = k * (1.0 + (a_sig - 1.0) * k_a)
    b_s = kk * a_sig
    yield

    chunks = range(n_chunks)
    rows = [slice(c * C, (c + 1) * C) for c in chunks]
    wl_hi = wl.astype(BF16)
    wl_r = wl - wl_hi.astype(F32)
    wl_mid = wl_r.astype(BF16)
    wl_lo = (wl_r - wl_mid.astype(F32)).astype(BF16)
    wl_parts = jnp.concatenate([wl_hi, wl_mid, wl_lo], axis=1)
    cum_parts = [_dot(tril_c, wl_parts[s]) for s in rows]
    cum_c = [m[:, :PAIR] + (m[:, PAIR:2 * PAIR] + m[:, 2 * PAIR:]) for m in cum_parts]
    cum = jnp.concatenate(cum_c, axis=0)
    cum_end = jnp.concatenate([jnp.broadcast_to(m[C - 1:C, :], (C, PAIR)) for m in cum_c], axis=0)
    e_neg = jnp.exp(-cum)
    e_tail = jnp.exp(cum_end - cum)
    r_t = r * jnp.exp(cum)
    a_t = -kk * jnp.exp(cum - wl)
    b_t = b_s * e_neg
    k_t = k2 * e_neg
    bw = b_s * e_tail
    kw = k2 * e_tail
    w_end = jnp.exp(cum_end)
    yield

    a_st = [stack_b(a_t[s]) for s in rows]
    r_st = [stack(r_t[s]) for s in rows]
    v_st = [stack_b(v[s]) for s in rows]
    gram = [bdot(jnp.concatenate([a_st[c], r_st[c].astype(BF16)], axis=0),
                 jnp.concatenate([stack_b(b_t[rows[c]]), stack_b(k_t[rows[c]])], axis=0), _NT)
            for c in chunks]
    yield
    a_ab =[jnp.where(strict, m[:PAIR, :PAIR], 0.0) for m in gram]
    a_ak = [jnp.where(strict, m[:PAIR, PAIR:], 0.0).astype(BF16) for m in gram]
    g_rb = [jnp.where(lower, m[PAIR:, :PAIR], 0.0).astype(BF16) for m in gram]
    g_rk = [jnp.where(lower, m[PAIR:, PAIR:], 0.0).astype(BF16) for m in gram]

    pw = [m.astype(BF16) for m in a_ab]
    pw = [_dot(m, m).astype(BF16) for m in pw]
    t_inv = [eye + m for m in a_ab]
    yield
    n = 2
    while 2 * n < C:
        both = [_dot(pw[c], jnp.concatenate([pw[c], t_inv[c].astype(BF16)], axis=1)) for c in chunks]
        pw = [m[:, :PAIR].astype(BF16) for m in both]
        t_inv = [t_inv[c] + both[c][:, PAIR:] for c in chunks]
        n *= 2
        yield
    t_b = [(t_inv[c] + _dot(pw[c], t_inv[c].astype(BF16))).astype(BF16) for c in chunks]

    a_v = [_dot(a_ak[c], v_st[c]).astype(BF16) for c in chunks]
    yield
    t_av =[_dot(t_b[c], jnp.concatenate([a_st[c], a_v[c]], axis=1)).astype(BF16)
            for c in chunks]
    zero = jnp.zeros((PAIR, PAIR), BF16)
    rhs2 = [jnp.concatenate([t_av[c], jnp.concatenate([zero, v_st[c]], axis=1)], axis=0)
            for c in chunks]
    yield
    bw_st = [stack_b(bw[s]) for s in rows]
    kw_st = [stack_b(kw[s]) for s in rows]
    mn = [_dot(jnp.concatenate([bw_st[c], kw_st[c]], axis=0), rhs2[c], _TN) for c in chunks]
    qz = [_dot(jnp.concatenate([g_rb[c], g_rk[c]], axis=1), rhs2[c]) for c in chunks]
    m_c = [(eye * w_end[rows[c]][0:1, :] + mn[c][:, :PAIR]).astype(BF16) for c in chunks]
    n_c = [mn[c][:, PAIR:] for c in chunks]
    q_c = [(r_st[c] + qz[c][:, :PAIR]).astype(BF16) for c in chunks]
    z_c = [qz[c][:, PAIR:] for c in chunks]
    yield

    pairs = range(n_chunks // 2)
    comp = [_dot(m_c[2 * j + 1], jnp.concatenate([m_c[2 * j], n_c[2 * j].astype(BF16)], axis=1))
            for j in pairs]
    m_2 = [comp[j][:, :PAIR].astype(BF16) for j in pairs]
    n_2 = [comp[j][:, PAIR:] + n_c[2 * j + 1] for j in pairs]
    yield
    h = h_ref[q]
    outs = []
    for j in pairs:
        c0 = 2 * j
        c1 = c0 + 1
        h_b = h.astype(BF16)
        from_h = _dot(jnp.concatenate([m_2[j], m_c[c0], q_c[c0]], axis=0), h_b)
        h = from_h[:PAIR, :] + n_2[j]
        h_mid = (from_h[PAIR:2 * PAIR, :] + n_c[c0]).astype(BF16)
        o_0 = from_h[2 * PAIR:, :] + z_c[c0]
        o_1 = _dot(q_c[c1], h_mid) + z_c[c1]
        outs.append(o_0[:C, :] + o_0[C:, :])
        outs.append(o_1[:C, :] + o_1[C:, :])
        yield
    h_ref[q] = h
    o = jnp.concatenate(outs, axis=0)

    inv_n = 1.0 / RWKV_HEAD
    mu = head_sum(o, 2) * inv_n
    d = o - mu
    var = head_sum(d * d, 2) * inv_n
    y = d * lax.rsqrt(var + GN_EPS) * gn_g + gn_b
    y = y + head_sum(r * k2 * r_k, 1) * v
    o_ref[:, lanes] = (y * g).astype(o_ref.dtype)


def _rwkv_kernel(*refs, n_chunks):
    o_ref, h_ref = refs[-2:]

    @pl.when(pl.program_id(2) == 0)
    def _():
        h_ref[...] = jnp.zeros_like(h_ref)

    live = []
    for q in range(o_ref.shape[1] // PAIR):
        live.append(_rwkv_pair_stages(q, *refs, n_chunks))
        for _ in range(3):
            live = [gen for gen in live if next(gen, _DONE) is not _DONE]
    while live:
        live = [gen for gen in live if next(gen, _DONE) is not _DONE]


def _rwkv_mix(zs, lora, w_up, a_up, g_up, prm, batch, seq, d_rwkv, tt=1024, pairs_per_step=2):
    M = zs.shape[0]
    wb = pairs_per_step * PAIR
    n_cols = d_rwkv // wb
    tt = _tile(seq, tt)
    nt = seq // tt
    zspec = lambda off: pl.BlockSpec((tt, wb), lambda b, p, t: (b * nt + t, off + p))
    wspec = lambda k: pl.BlockSpec((k, wb), lambda b, p, t: (0, p))
    head_of = jnp.arange(PAIR) // RWKV_HEAD
    same_head = (head_of[:, None] == head_of[None, :]).astype(BF16)
    return pl.pallas_call(
        functools.partial(_rwkv_kernel, n_chunks=tt // CHUNK),
        out_shape=jax.ShapeDtypeStruct((M, d_rwkv), BF16),
        grid=(batch, n_cols, nt),
        in_specs=[
            zspec(0), zspec(n_cols), zspec(2 * n_cols),
            pl.BlockSpec((tt, 4 * LANES), lambda b, p, t: (b * nt + t, 0)),
            wspec(LANES), wspec(LANES), wspec(2 * LANES), wspec(8),
            pl.BlockSpec((PAIR, PAIR), lambda b, p, t: (0, 0)),
        ],
        out_specs=pl.BlockSpec((tt, wb), lambda b, p, t: (b * nt + t, p)),
        scratch_shapes=[pltpu.VMEM((pairs_per_step, PAIR, PAIR), F32)],
        compiler_params=_params("parallel", "parallel", "arbitrary"),
    )(zs, zs, zs, lora, w_up, a_up, g_up, prm, same_head)


def _conv_kernel(u_ref, w_ref, cb_ref, g_ref, b_ref, o_ref, buf_ref, acc_ref, win_ref, *, width):
    tt, ch = u_ref.shape
    t = pl.program_id(1)

    @pl.when(t == 0)
    def _():
        buf_ref[0:CONV_HALO, :] = jnp.zeros((CONV_HALO, ch), F32)

    buf_ref[CONV_HALO:, :] = u_ref[...]
    base = CONV_HALO - (width - 1)

    sub = 8

    def slab(s, _):
        cols = pl.ds(pl.multiple_of(s * LANES, LANES), LANES)
        acc = jnp.broadcast_to(cb_ref[:, cols], (tt, LANES))
        for res in range(sub):
            taps = [j for j in range(width) if (base + j) % sub == res]
            if not taps:
                continue
            span = (base + taps[-1]) // sub * sub
            win_ref[0:tt + span, :] = buf_ref[pl.ds(res, tt + span), cols]
            for j in taps:
                off = base + j - res
                acc = acc + w_ref[pl.ds(j, 1), cols] * win_ref[off:off + tt, :]
        acc_ref[:, cols] = acc
        return 0

    lax.fori_loop(0, ch // LANES, slab, 0)
    buf_ref[0:CONV_HALO, :] = buf_ref[tt:tt + CONV_HALO, :]

    g = g_ref[...]
    b = b_ref[...]
    rows = min(tt, 128)

    def norm(i, _):
        sl = pl.ds(pl.multiple_of(i * rows, rows), rows)
        y = _layer_norm_rows(acc_ref[sl, :], g, b)
        o_ref[sl, :] = (y * _sigmoid(y)).astype(o_ref.dtype)
        return 0

    lax.fori_loop(0, tt // rows, norm, 0)


def _conv_module(u, conv_w, conv_b, ln_g, ln_b, batch, seq, width, tt=256):
    M, ch = u.shape
    tt = _tile(seq, tt)
    nt = seq // tt
    vec = lambda r: pl.BlockSpec((r, ch), lambda b, t: (0, 0))
    return pl.pallas_call(
        functools.partial(_conv_kernel, width=width),
        out_shape=jax.ShapeDtypeStruct((M, ch), BF16),
        grid=(batch, nt),
        in_specs=[pl.BlockSpec((tt, ch), lambda b, t: (b * nt + t, 0)),
                  vec(conv_w.shape[0]), vec(1), vec(1), vec(1)],
        out_specs=pl.BlockSpec((tt, ch), lambda b, t: (b * nt + t, 0)),
        scratch_shapes=[pltpu.VMEM((tt + CONV_HALO, ch), F32), pltpu.VMEM((tt, ch), F32),
                        pltpu.VMEM((tt + CONV_HALO, LANES), F32)],
        compiler_params=_params("parallel", "arbitrary"),
    )(u, conv_w, conv_b, ln_g, ln_b)


def _qk_fold_kernel(wq_ref, k_ref, o_ref):
    o_ref[...] = _dot(wq_ref[...], k_ref[...], _NT).astype(o_ref.dtype)


def _qk_fold(wq, kx, batch, mem_len, heads):
    D = wq.shape[0]
    hd = D // heads
    return pl.pallas_call(
        _qk_fold_kernel,
        out_shape=jax.ShapeDtypeStruct((batch, D, heads * mem_len), BF16),
        grid=(heads, batch),
        in_specs=[pl.BlockSpec((D, hd), lambda h, b: (0, h)),
                  pl.BlockSpec((mem_len, hd), lambda h, b: (b, h))],
        out_specs=pl.BlockSpec((None, D, mem_len), lambda h, b: (b, 0, h)),
        compiler_params=_params("parallel", "arbitrary"),
    )(wq, kx)


def _vo_fold_kernel(v_ref, wo_ref, o_ref):
    o_ref[...] = _dot(v_ref[...], wo_ref[...]).astype(o_ref.dtype)


def _vo_fold(vx, wo, batch, mem_len, heads):
    D = wo.shape[0]
    hd = D // heads
    return pl.pallas_call(
        _vo_fold_kernel,
        out_shape=jax.ShapeDtypeStruct((batch, heads * mem_len, D), BF16),
        grid=(heads, batch),
        in_specs=[pl.BlockSpec((mem_len, hd), lambda h, b: (b, h)),
                  pl.BlockSpec((hd, D), lambda h, b: (h, 0))],
        out_specs=pl.BlockSpec((None, mem_len, D), lambda h, b: (b, h, 0)),
        compiler_params=_params("parallel", "arbitrary"),
    )(vx, wo)


def _attn_ln_kernel(x_ref, qk_ref, vo_ref, res_ref, g_ref, b_ref, o_ref, ob_ref, *, heads, scale, alpha):
    s = _dot(x_ref[...], qk_ref[...]) * scale
    mem_len = s.shape[1] // heads
    probs = []
    for h in range(heads):
        sh = s[:, h * mem_len:(h + 1) * mem_len]
        e = jnp.exp(sh - jnp.max(sh, axis=-1, keepdims=True))
        probs.append((e / jnp.sum(e, axis=-1, keepdims=True)).astype(BF16))
    p = jnp.concatenate(probs, axis=1)
    o_ref[...] = alpha * res_ref[...] + _dot(p, vo_ref[...])
    _ln_inplace(o_ref, ob_ref, g_ref, b_ref)


def _attn_ln(x, qk, vo, res, g, b, alpha, batch, seq, heads, tq=256):
    M, D = x.shape
    hm = qk.shape[2]
    tq = _tile(seq, tq)
    nt = seq // tq
    scale = float((D // heads) ** -0.5)
    single = pl.Buffered(1)
    tok = lambda: pl.BlockSpec((tq, D), lambda bi, t: (bi * nt + t, 0))
    vec = pl.BlockSpec((1, D), lambda bi, t: (0, 0))
    return pl.pallas_call(
        functools.partial(_attn_ln_kernel, heads=heads, scale=scale, alpha=alpha),
        out_shape=(jax.ShapeDtypeStruct((M, D), F32), jax.ShapeDtypeStruct((M, D), BF16)),
        grid=(batch, nt),
        in_specs=[tok(),
                  pl.BlockSpec((None, D, hm), lambda bi, t: (bi, 0, 0), pipeline_mode=single),
                  pl.BlockSpec((None, hm, D), lambda bi, t: (bi, 0, 0), pipeline_mode=single),
                  tok(), vec, vec],
        out_specs=(tok(), tok()),
        compiler_params=_params("parallel", "arbitrary"),
    )(x, qk, vo, res, g, b)


def _pad_cols(w, n):
    return jnp.pad(w, ((0, 0), (0, n - w.shape[1])))


def _pad_rows(w, n):
    return jnp.pad(w, ((0, n - w.shape[0]), (0, 0)))


def kernel(x, mem, w_in, rwkv_shift_mix, rwkv_w0, rwkv_w_up, rwkv_a0, rwkv_a_up, rwkv_g_up, rwkv_k_k, rwkv_k_a, rwkv_r_k, rwkv_gn_g, rwkv_gn_b, conv_w, conv_b, conv_ln_g, conv_ln_b, proj_rwkv, proj_conv, w_out, ln1_g, ln1_b, ln_mem_g, ln_mem_b, xattn_wq, xattn_wk, xattn_wv, xattn_wo, ln2_g, ln2_b, mlp_w1, mlp_w2, ln3_g, ln3_b):
    B, S, D = x.shape
    depth = w_in.shape[0]
    mem_len = mem.shape[1]
    d_rwkv = rwkv_w0.shape[1]
    d_conv = conv_b.shape[1]
    r_decay = rwkv_w_up.shape[1]
    r_iclr = rwkv_a_up.shape[1]
    r_gate = rwkv_g_up.shape[1]
    width = conv_w.shape[1]
    assert r_decay <= LANES and r_iclr <= LANES and r_gate == 2 * LANES
    assert d_rwkv % PAIR == 0 and width - 1 <= CONV_HALO
    alpha = float((2 * depth) ** 0.25)
    M = B * S
    row = lambda a: a.reshape(1, -1)

    mem_n = _ln_rows(mem.reshape(B * mem_len, D), row(ln_mem_g), row(ln_mem_b), BF16)
    h = x.reshape(M, D)
    hb = h.astype(BF16)
    for l in range(depth):
        w = w_in[l]
        c_lora = 3 * d_rwkv
        c_conv = c_lora + r_decay + r_iclr + r_gate
        c_gate = c_conv + 2 * d_conv
        mix = rwkv_shift_mix[l]
        zs = _mm_shift(hb, w[:, :c_lora].astype(BF16), row(mix[:c_lora]), S)
        w_lora = jnp.concatenate([
            _pad_cols(w[:, c_lora:c_lora + r_decay], LANES),
            _pad_cols(w[:, c_lora + r_decay:c_lora + r_decay + r_iclr], LANES),
            w[:, c_lora + r_decay + r_iclr:c_conv]], axis=1).astype(BF16)
        mu_lora = jnp.concatenate([
            jnp.pad(mix[c_lora:c_lora + r_decay], (0, LANES - r_decay)),
            jnp.pad(mix[c_lora + r_decay:c_lora + r_decay + r_iclr], (0, LANES - r_iclr)),
            mix[c_lora + r_decay + r_iclr:]]).reshape(1, -1)
        lora = _lora_act(_mm_shift(hb, w_lora, mu_lora, S), 0)
        prm = jnp.stack([rwkv_w0[l], rwkv_a0[l], rwkv_k_k[l], rwkv_k_a[l], rwkv_r_k[l].reshape(-1),
                         rwkv_gn_g[l], rwkv_gn_b[l], jnp.zeros((d_rwkv,), F32)])
        o_r = _rwkv_mix(zs, lora, _pad_rows(rwkv_w_up[l], LANES), _pad_rows(rwkv_a_up[l], LANES),
                        rwkv_g_up[l], prm, B, S, d_rwkv)

        u = _mm_glu(hb, w[:, c_conv:c_conv + d_conv].astype(BF16),
                    w[:, c_conv + d_conv:c_gate].astype(BF16))
        cw = _pad_rows(conv_w[l].reshape(width, d_conv), CONV_HALO)
        o_c = _conv_module(u, cw, row(conv_b[l]), row(conv_ln_g[l]), row(conv_ln_b[l]), B, S, width)

        merged = _mm_merge(hb, o_r, o_c, w[:, c_gate:c_gate + D].astype(BF16),
                           w[:, c_gate + D:].astype(BF16),
                           proj_rwkv[l].astype(BF16), proj_conv[l].astype(BF16))
        h, hb = _mm_res_ln(merged, w_out[l], h, row(ln1_g[l]), row(ln1_b[l]), alpha)

        kx = _mm_plain(mem_n, xattn_wk[l].astype(BF16), BF16)
        vx = _mm_plain(mem_n, xattn_wv[l].astype(BF16), BF16)
        qk = _qk_fold(xattn_wq[l].astype(BF16), kx, B, mem_len, XATTN_HEADS)
        vo = _vo_fold(vx, xattn_wo[l].astype(BF16), B, mem_len, XATTN_HEADS)
        h, hb = _attn_ln(hb, qk, vo, h, row(ln2_g[l]), row(ln2_b[l]), alpha, B, S, XATTN_HEADS)

        h = _mlp_ln(hb, mlp_w1[l], mlp_w2[l], h, row(ln3_g[l]), row(ln3_b[l]), alpha)
        hb = h.astype(BF16)
    return h.reshape(B, S, D)
```
